```python
import jax
import jax.numpy as jnp
from jax import lax
import numpy as np

D_MODEL = 1024
BATCH = 32
SEQ = 2048
DEPTH = 2

CTX_LEN = 256
GRID_W = 64
ROPE_THETA = 10000.0
RMS_EPS = 1e-6
QBLK = 128
NEG_INF = -1e30
N_BRANCH = 3

MLA_HEADS = 8
MLA_NOPE = 64
MLA_ROPE = 32
MLA_V = 64
MLA_Q_RANK = 384
MLA_KV_RANK = 256
MLA_WIDTH = MLA_HEADS * MLA_V
MLA_SCALE = (MLA_NOPE + MLA_ROPE) ** -0.5

SWA_HEADS = 8
SWA_KV_HEADS = 2
SWA_GROUP = SWA_HEADS // SWA_KV_HEADS
SWA_HEAD_DIM = 64
SWA_WIDTH = SWA_HEADS * SWA_HEAD_DIM
SWA_KV_WIDTH = SWA_KV_HEADS * SWA_HEAD_DIM
WINDOW = 128
SWA_BAND = QBLK + 2 * WINDOW
SWA_SCALE = SWA_HEAD_DIM ** -0.5

AX_HEADS = 8
AX_KV_HEADS = 2
AX_GROUP = AX_HEADS // AX_KV_HEADS
AX_HEAD_DIM = 64
AX_WIDTH = AX_HEADS * AX_HEAD_DIM
AX_KV_WIDTH = AX_KV_HEADS * AX_HEAD_DIM
AX_SCALE = AX_HEAD_DIM ** -0.5

KV_SPLITS = (MLA_KV_RANK, MLA_ROPE, SWA_KV_WIDTH, SWA_KV_WIDTH, AX_KV_WIDTH, AX_KV_WIDTH)
Q_SPLITS = (MLA_Q_RANK, SWA_WIDTH, AX_WIDTH, MLA_WIDTH, SWA_WIDTH, AX_WIDTH, N_BRANCH * D_MODEL)
KV_COLS = sum(KV_SPLITS)
IN_WIDTH = KV_COLS + sum(Q_SPLITS)

kernel_name = 'hybrid_mla_swa_axial_parallel_dit_block'


def _split(t, sizes):
    offs = np.cumsum(sizes)[:-1].tolist()
    return jnp.split(t, offs, axis=-1)


def rmsnorm(x, w):
    xf = x.astype(jnp.float32)
    y = xf * lax.rsqrt(jnp.mean(xf * xf, axis=-1, keepdims=True) + RMS_EPS)
    return (y * w.astype(jnp.float32)).astype(x.dtype)


def rope_1d(x, pos):
    d = x.shape[-1]
    T = x.shape[1]
    freqs = ROPE_THETA ** (-jnp.arange(0, d, 2, dtype=jnp.float32) / d)
    ang = pos.astype(jnp.float32)[:, None] * freqs[None, :]
    bshape = (T,) + (1,) * (x.ndim - 3) + (d // 2,)
    cos = jnp.cos(ang).reshape(bshape)
    sin = jnp.sin(ang).reshape(bshape)
    xf = x.astype(jnp.float32)
    x1, x2 = xf[..., : d // 2], xf[..., d // 2:]
    return jnp.concatenate([x1 * cos - x2 * sin, x1 * sin + x2 * cos], axis=-1).astype(x.dtype)


def rope_axial(x, pos_row, pos_col):
    h = x.shape[-1] // 2
    return jnp.concatenate([rope_1d(x[..., :h], pos_row), rope_1d(x[..., h:], pos_col)], axis=-1)


def _rope_fn(pos):
    if pos is None:
        return lambda t: t
    return lambda t: rope_axial(t, pos[0], pos[1])


def attend(q, k, v, scale, mask=None, sink=None):
    s = jnp.einsum('bqhgd,bkhd->bhgqk', q.astype(jnp.float32), k.astype(jnp.float32)) * scale
    if mask is not None:
        s = jnp.where(mask, s, NEG_INF)
    if sink is not None:
        sk = jnp.broadcast_to(sink.astype(jnp.float32)[None, :, :, None, None], s.shape[:-1] + (1,))
        p = jax.nn.softmax(jnp.concatenate([s, sk], axis=-1), axis=-1)[..., :-1]
    else:
        p = jax.nn.softmax(s, axis=-1)
    o = jnp.einsum('bhgqk,bkhd->bqhgd', p, v.astype(jnp.float32))
    return o.astype(v.dtype)


def dense_latent_attention(q, k, v, k_ctx, v_ctx, scale):
    B, T, Hkv, G, Dk = q.shape
    nb = T // QBLK
    kk = jnp.concatenate([k_ctx, k], axis=1)
    vv = jnp.concatenate([v_ctx, v], axis=1)
    qb = jnp.moveaxis(q.reshape(B, nb, QBLK, Hkv, G, Dk), 1, 0)
    ob = lax.map(lambda qi: attend(qi, kk, vv, scale), qb)
    return jnp.moveaxis(ob, 0, 1).reshape(B, T, Hkv, G, -1)


def window_latent_attention(q, k, v, k_ctx, v_ctx, sink, scale):
    B, T, Hkv, G, Dk = q.shape
    nb = T // QBLK
    Lc = k_ctx.shape[1]
    pad = ((0, 0), (WINDOW, WINDOW), (0, 0), (0, 0))
    kp = jnp.pad(k, pad)
    vp = jnp.pad(v, pad)
    qb = jnp.moveaxis(q.reshape(B, nb, QBLK, Hkv, G, Dk), 1, 0)
    k_off = jnp.arange(SWA_BAND, dtype=jnp.int32) - WINDOW
    rel = k_off[None, :] - jnp.arange(QBLK, dtype=jnp.int32)[:, None]
    local = jnp.abs(rel) <= WINDOW
    ctx_mask = jnp.ones((QBLK, Lc), dtype=bool)

    def block(args):
        i, qi = args
        start = i * QBLK
        kb = lax.dynamic_slice_in_dim(kp, start, SWA_BAND, axis=1)
        vb = lax.dynamic_slice_in_dim(vp, start, SWA_BAND, axis=1)
        k_abs = start + k_off
        valid = (k_abs >= 0) & (k_abs < T)
        mask = jnp.concatenate([ctx_mask, local & valid[None, :]], axis=1)
        kk = jnp.concatenate([k_ctx, kb], axis=1)
        vv = jnp.concatenate([v_ctx, vb], axis=1)
        return attend(qi, kk, vv, scale, mask=mask, sink=sink)

    ob = lax.map(block, (jnp.arange(nb, dtype=jnp.int32), qb))
    return jnp.moveaxis(ob, 0, 1).reshape(B, T, Hkv, G, -1)


def project_kv(p, lw, pos):
    B, T, _ = p.shape
    rope = _rope_fn(pos)
    ckv, kr, sk, sv, ak, av = _split(p, KV_SPLITS)
    kv_m = (rmsnorm(ckv, lw['mla_kv_norm']) @ lw['mla_w_ukv']).reshape(B, T, MLA_HEADS, MLA_NOPE + MLA_V)
    k_rope = rope(kr.reshape(B, T, 1, MLA_ROPE))
    k_m = jnp.concatenate([kv_m[..., :MLA_NOPE], jnp.broadcast_to(k_rope, (B, T, MLA_HEADS, MLA_ROPE))], axis=-1)
    v_m = kv_m[..., MLA_NOPE:]
    k_s = rope(sk.reshape(B, T, SWA_KV_HEADS, SWA_HEAD_DIM))
    v_s = sv.reshape(B, T, SWA_KV_HEADS, SWA_HEAD_DIM)
    k_a = rope(rmsnorm(ak.reshape(B, T, AX_KV_HEADS, AX_HEAD_DIM), lw['ax_k_norm']))
    v_a = av.reshape(B, T, AX_KV_HEADS, AX_HEAD_DIM)
    return k_m, v_m, k_s, v_s, k_a, v_a


def project_q(p, lw, pos):
    B, T, _ = p.shape
    rope = _rope_fn(pos)
    cq, sq, aq, z_m, z_s, z_a, g = _split(p, Q_SPLITS)
    q_m = (rmsnorm(cq, lw['mla_q_norm']) @ lw['mla_w_uq']).reshape(B, T, MLA_HEADS, 1, MLA_NOPE + MLA_ROPE)
    q_m = jnp.concatenate([q_m[..., :MLA_NOPE], rope(q_m[..., MLA_NOPE:])], axis=-1)
    q_s = rope(sq.reshape(B, T, SWA_KV_HEADS, SWA_GROUP, SWA_HEAD_DIM))
    q_a = rope(rmsnorm(aq.reshape(B, T, AX_KV_HEADS, AX_GROUP, AX_HEAD_DIM), lw['ax_q_norm']))
    return q_m, q_s, q_a, z_m, z_s, z_a, g


def merge_branches(o_m, o_s, o_a, z_m, z_s, z_a, g, lw):
    B, T = o_m.shape[:2]
    y_m = (o_m.reshape(B, T, MLA_WIDTH) * jax.nn.silu(z_m)) @ lw['w_o_mla']
    y_s = (o_s.reshape(B, T, SWA_WIDTH) * jax.nn.silu(z_s)) @ lw['w_o_swa']
    y_a = (o_a.reshape(B, T, AX_WIDTH) * jax.nn.silu(z_a)) @ lw['w_o_ax']
    g_m, g_s, g_a = jnp.split(g, N_BRANCH, axis=-1)
    y = jax.nn.sigmoid(g_m) * y_m + jax.nn.sigmoid(g_s) * y_s + jax.nn.sigmoid(g_a) * y_a
    return y @ lw['w_out']


def hybrid_layer(x, ctx, c, c_ctx, lw, pos, update_ctx):
    mod_x = jax.nn.silu(c) @ lw['ada_w'] + lw['ada_b']
    mod_c = jax.nn.silu(c_ctx) @ lw['ada_w'] + lw['ada_b']
    shift, scale, gate = jnp.split(mod_x[:, None, :], 3, axis=-1)
    c_shift, c_scale, c_gate = jnp.split(mod_c, 3, axis=-1)
    h = rmsnorm(x, lw['norm_w']) * (1 + scale) + shift
    hc = rmsnorm(ctx, lw['norm_w']) * (1 + c_scale) + c_shift

    p = h @ lw['w_in']
    k_m, v_m, k_s, v_s, k_a, v_a = project_kv(p[..., :KV_COLS], lw, pos)
    q_m, q_s, q_a, z_m, z_s, z_a, g = project_q(p[..., KV_COLS:], lw, pos)
    kc_m, vc_m, kc_s, vc_s, kc_a, vc_a = project_kv(hc @ lw['w_in'][:, :KV_COLS], lw, None)

    sink = lw['swa_sink'].reshape(SWA_KV_HEADS, SWA_GROUP)
    o_m = dense_latent_attention(q_m, k_m, v_m, kc_m, vc_m, MLA_SCALE)
    o_s = window_latent_attention(q_s, k_s, v_s, kc_s, vc_s, sink, SWA_SCALE)
    o_a = dense_latent_attention(q_a, k_a, v_a, kc_a, vc_a, AX_SCALE)
    x = x + gate * merge_branches(o_m, o_s, o_a, z_m, z_s, z_a, g, lw)

    if update_ctx:
        qc_m, qc_s, qc_a, zc_m, zc_s, zc_a, gc = project_q(hc @ lw['w_in'][:, KV_COLS:], lw, None)
        oc_m = attend(qc_m, kc_m, vc_m, MLA_SCALE)
        oc_s = attend(qc_s, kc_s, vc_s, SWA_SCALE, sink=sink)
        oc_a = attend(qc_a, kc_a, vc_a, AX_SCALE)
        ctx = ctx + c_gate * merge_branches(oc_m, oc_s, oc_a, zc_m, zc_s, zc_a, gc, lw)
    return x, ctx


def setup_inputs(seed: int = 0) -> dict:
    key = jax.random.key(seed)
    ks = jax.random.split(key, 20)
    D = D_MODEL
    f32 = jnp.float32

    def nrm(k, shape, scale):
        return jax.random.normal(k, shape, f32) * scale

    return {
        'x': nrm(ks[0], (BATCH, SEQ, D), 1.0),
        'c': nrm(ks[1], (BATCH, D), 1.0),
        'ctx': nrm(ks[2], (BATCH, CTX_LEN, D), 1.0),
        'c_ctx': nrm(ks[3], (D,), 1.0),
        'ada_w': nrm(ks[4], (DEPTH, D, 3 * D), 0.5 * D ** -0.5),
        'ada_b': nrm(ks[5], (DEPTH, 3 * D), 0.02),
        'norm_w': 1.0 + nrm(ks[6], (DEPTH, D), 0.1),
        'w_in': nrm(ks[7], (DEPTH, D, IN_WIDTH), D ** -0.5),
        'mla_q_norm': 1.0 + nrm(ks[8], (DEPTH, MLA_Q_RANK), 0.1),
        'mla_w_uq': nrm(ks[9], (DEPTH, MLA_Q_RANK, MLA_HEADS * (MLA_NOPE + MLA_ROPE)), MLA_Q_RANK ** -0.5),
        'mla_kv_norm': 1.0 + nrm(ks[10], (DEPTH, MLA_KV_RANK), 0.1),
        'mla_w_ukv': nrm(ks[11], (DEPTH, MLA_KV_RANK, MLA_HEADS * (MLA_NOPE + MLA_V)), MLA_KV_RANK ** -0.5),
        'swa_sink': nrm(ks[12], (DEPTH, SWA_HEADS), 0.5),
        'ax_q_norm': 1.0 + nrm(ks[13], (DEPTH, AX_HEAD_DIM), 0.1),
        'ax_k_norm': 1.0 + nrm(ks[14], (DEPTH, AX_HEAD_DIM), 0.1),
        'w_o_mla': nrm(ks[15], (DEPTH, MLA_WIDTH, D), MLA_WIDTH ** -0.5),
        'w_o_swa': nrm(ks[16], (DEPTH, SWA_WIDTH, D), SWA_WIDTH ** -0.5),
        'w_o_ax': nrm(ks[17], (DEPTH, AX_WIDTH, D), AX_WIDTH ** -0.5),
        'w_out': nrm(ks[18], (DEPTH, D, D), D ** -0.5),
        'final_norm_w': 1.0 + nrm(ks[19], (D,), 0.1),
    }


def reference(x, c, ctx, c_ctx, ada_w, ada_b, norm_w, w_in, mla_q_norm, mla_w_uq, mla_kv_norm, mla_w_ukv,
              swa_sink, ax_q_norm, ax_k_norm, w_o_mla, w_o_swa, w_o_ax, w_out, final_norm_w):
    T = x.shape[1]
    rows = T // GRID_W
    pos_row = jnp.repeat(jnp.arange(rows, dtype=jnp.int32), GRID_W)
    pos_col = jnp.tile(jnp.arange(GRID_W, dtype=jnp.int32), rows)
    pos = (pos_row, pos_col)
    for l in range(DEPTH):
        lw = {
            'ada_w': ada_w[l], 'ada_b': ada_b[l], 'norm_w': norm_w[l], 'w_in': w_in[l],
            'mla_q_norm': mla_q_norm[l], 'mla_w_uq': mla_w_uq[l],
            'mla_kv_norm': mla_kv_norm[l], 'mla_w_ukv': mla_w_ukv[l],
            'swa_sink': swa_sink[l], 'ax_q_norm': ax_q_norm[l], 'ax_k_norm': ax_k_norm[l],
            'w_o_mla': w_o_mla[l], 'w_o_swa': w_o_swa[l], 'w_o_ax': w_o_ax[l], 'w_out': w_out[l],
        }
        x, ctx = hybrid_layer(x, ctx, c, c_ctx, lw, pos, update_ctx=(l < DEPTH - 1))
    return rmsnorm(x, final_norm_w)
```

```python
import functools

import jax
import jax.numpy as jnp
from jax import lax
from jax.experimental import pallas as pl
from jax.experimental.pallas import tpu as pltpu

D_MODEL = 1024
CTX_LEN = 256
GRID_W = 64
ROPE_THETA = 10000.0
RMS_EPS = 1e-6
NEG_INF = -1e30
N_BRANCH = 3
WINDOW = 128

HEADS = 8
KV_HEADS = 2
HEAD_DIM = 64
MLA_ROPE = 32
MLA_Q_RANK = 384
MLA_KV_RANK = 256
MLA_SCALE = (HEAD_DIM + MLA_ROPE) ** -0.5
GQA_SCALE = HEAD_DIM ** -0.5
BRANCH_W = HEADS * HEAD_DIM

LANES = 128
N_PAIRS = BRANCH_W // LANES
MLA_QK_W = 2 * LANES

TOK_TILE = 256
Q_TILE = 256
BAND = Q_TILE + 2 * WINDOW

KV_W = MLA_KV_RANK + 5 * LANES
Q_W = MLA_Q_RANK + 5 * BRANCH_W + N_BRANCH * D_MODEL
Z_W = 3 * BRANCH_W
G_W = N_BRANCH * D_MODEL
TAB_W = 12 * LANES

VMEM_LIMIT = 56 * 1024 * 1024

_F32 = jnp.float32
_BF16 = jnp.bfloat16


def _params(n_axes):
    return pltpu.CompilerParams(
        dimension_semantics=("arbitrary",) * n_axes, vmem_limit_bytes=VMEM_LIMIT)


def _const_spec(shape):
    nd = len(shape)
    return pl.BlockSpec(shape, lambda *_: (0,) * nd, pipeline_mode=pl.Buffered(1))


def _mod_kernel(c_ref, w_ref, b_ref, o_ref):
    c = c_ref[...]
    s = c / (1.0 + jnp.exp(-c))
    o_ref[0] = jnp.dot(s, w_ref[0], preferred_element_type=_F32,
                       precision=lax.Precision.HIGHEST) + b_ref[0]


def _modulation(cc, ada_w, ada_b):
    depth = ada_w.shape[0]
    rows = cc.shape[0]
    col_tile = D_MODEL
    return pl.pallas_call(
        _mod_kernel,
        grid=(depth, (3 * D_MODEL) // col_tile),
        in_specs=[
            pl.BlockSpec((rows, D_MODEL), lambda l, n: (0, 0)),
            pl.BlockSpec((1, D_MODEL, col_tile), lambda l, n: (l, 0, n)),
            pl.BlockSpec((1, 1, col_tile), lambda l, n: (l, 0, n)),
        ],
        out_specs=pl.BlockSpec((1, rows, col_tile), lambda l, n: (l, 0, n)),
        out_shape=jax.ShapeDtypeStruct((depth, rows, 3 * D_MODEL), _F32),
        compiler_params=_params(2),
        name="modulation",
    )(cc, ada_w, ada_b.reshape(depth, 1, 3 * D_MODEL))


def _rms(x, w):
    ms = jnp.mean(x * x, axis=-1, keepdims=True)
    return x * lax.rsqrt(ms + RMS_EPS) * w


def _head_rms(x, w):
    lane = lax.broadcasted_iota(jnp.int32, x.shape, 1)
    y = x * x
    s = 1
    while s < HEAD_DIM:
        up = pltpu.roll(y, s, 1)
        dn = pltpu.roll(y, LANES - s, 1)
        y = y + jnp.where((lane & s) != 0, up, dn)
        s *= 2
    return x * lax.rsqrt(y * (1.0 / HEAD_DIM) + RMS_EPS) * w


def _rope(x, tab_ref, k, r):
    cos = tab_ref[:, (3 * k) * LANES:(3 * k + 1) * LANES]
    sinm = tab_ref[:, (3 * k + 1) * LANES:(3 * k + 2) * LANES]
    sinp = tab_ref[:, (3 * k + 2) * LANES:(3 * k + 3) * LANES]
    return x * cos + pltpu.roll(x, LANES - r, 1) * sinm + pltpu.roll(x, r, 1) * sinp


_TAB_K64, _TAB_Q64, _TAB_K32, _TAB_Q32 = 0, 1, 2, 3


def _proj_kernel(x_ref, ctx_ref, modx_ref, modc_ref, nw_ref, tab_ref,
                 wkv_ref, wq_ref, wuq_ref, wukv_ref, qn_ref, kvn_ref, axq_ref, axk_ref,
                 km_ref, vm_ref, ks_ref, vs_ref, ka_ref, va_ref,
                 qm_ref, qs_ref, qa_ref, z_ref, sg_ref, h_scr, *, update_ctx):
    j = pl.program_id(1)

    def norm_mod(xin, mod):
        shift = mod[:, :D_MODEL]
        scale = mod[:, D_MODEL:2 * D_MODEL]
        h = _rms(xin, nw_ref[...]) * (1.0 + scale) + shift
        h_scr[...] = h.astype(_BF16)

    @pl.when(j == 0)
    def _():
        norm_mod(ctx_ref[0], modc_ref[...])

    @pl.when(j > 0)
    def _():
        norm_mod(x_ref[0], modx_ref[0])

    h = h_scr[...]

    pkv = jnp.dot(h, wkv_ref[...], preferred_element_type=_F32)
    ckvn = _rms(pkv[:, :MLA_KV_RANK], kvn_ref[...]).astype(_BF16)
    kvm = jnp.dot(ckvn, wukv_ref[...], preferred_element_type=_F32)
    vm_ref[0] = kvm[:, BRANCH_W:].astype(_BF16)
    o = MLA_KV_RANK
    kr2 = _rope(pkv[:, o:o + LANES], tab_ref, _TAB_K32, MLA_ROPE // 4).astype(_BF16)
    for p in range(N_PAIRS):
        km_ref[0, :, p * MLA_QK_W:p * MLA_QK_W + LANES] = (
            kvm[:, p * LANES:(p + 1) * LANES].astype(_BF16))
        km_ref[0, :, p * MLA_QK_W + LANES:(p + 1) * MLA_QK_W] = kr2
    o += LANES
    ks_ref[0] = _rope(pkv[:, o:o + LANES], tab_ref, _TAB_K64, HEAD_DIM // 4).astype(_BF16)
    o += LANES
    vs_ref[0] = pkv[:, o:o + LANES].astype(_BF16)
    o += LANES
    ak = _head_rms(pkv[:, o:o + LANES], axk_ref[...])
    ka_ref[0] = _rope(ak, tab_ref, _TAB_K64, HEAD_DIM // 4).astype(_BF16)
    o += LANES
    va_ref[0] = pkv[:, o:o + LANES].astype(_BF16)

    def q_side():
        cq = jnp.dot(h, wq_ref[:, :MLA_Q_RANK], preferred_element_type=_F32)
        cqn = _rms(cq, qn_ref[...]).astype(_BF16)
        qm = jnp.dot(cqn, wuq_ref[...], preferred_element_type=_F32)
        for p in range(N_PAIRS):
            lo = p * MLA_QK_W
            qm_ref[0, :, lo:lo + LANES] = (qm[:, lo:lo + LANES] * MLA_SCALE).astype(_BF16)
            qm_ref[0, :, lo + LANES:lo + MLA_QK_W] = _rope(
                qm[:, lo + LANES:lo + MLA_QK_W], tab_ref, _TAB_Q32, MLA_ROPE // 4).astype(_BF16)
        c0 = MLA_Q_RANK
        sq = jnp.dot(h, wq_ref[:, c0:c0 + BRANCH_W], preferred_element_type=_F32)
        c0 += BRANCH_W
        aq = jnp.dot(h, wq_ref[:, c0:c0 + BRANCH_W], preferred_element_type=_F32)
        c0 += BRANCH_W
        for p in range(N_PAIRS):
            blk = slice(p * LANES, (p + 1) * LANES)
            qs_ref[0, :, blk] = _rope(sq[:, blk], tab_ref, _TAB_Q64, HEAD_DIM // 4).astype(_BF16)
            aqn = _head_rms(aq[:, blk], axq_ref[...])
            qa_ref[0, :, blk] = _rope(aqn, tab_ref, _TAB_Q64, HEAD_DIM // 4).astype(_BF16)
        for n in range(Z_W // BRANCH_W):
            z = jnp.dot(h, wq_ref[:, c0:c0 + BRANCH_W], preferred_element_type=_F32)
            z_ref[0, :, n * BRANCH_W:(n + 1) * BRANCH_W] = (z / (1.0 + jnp.exp(-z))).astype(_BF16)
            c0 += BRANCH_W
        for n in range(G_W // BRANCH_W):
            g = jnp.dot(h, wq_ref[:, c0:c0 + BRANCH_W], preferred_element_type=_F32)
            sg_ref[0, :, n * BRANCH_W:(n + 1) * BRANCH_W] = (1.0 / (1.0 + jnp.exp(-g))).astype(_BF16)
            c0 += BRANCH_W

    if update_ctx:
        q_side()
    else:
        pl.when(j > 0)(q_side)


def _project(x, ctx, modx, modc, norm_w, tab, wkv, wq, wuq, wukv, qn, kvn, axq, axk, *, update_ctx):
    B, T, D = x.shape
    n_tok = CTX_LEN + T
    nt = n_tok // TOK_TILE
    tq = n_tok if update_ctx else T

    def x_map(b, j):
        return (b, jnp.maximum(j - 1, 0), 0)

    q_map = (lambda b, j: (b, j, 0)) if update_ctx else x_map

    def kv_out(w):
        return (pl.BlockSpec((1, TOK_TILE, w), lambda b, j: (b, j, 0)),
                jax.ShapeDtypeStruct((B, n_tok, w), _BF16))

    def q_out(w):
        return (pl.BlockSpec((1, TOK_TILE, w), q_map),
                jax.ShapeDtypeStruct((B, tq, w), _BF16))

    outs = [kv_out(N_PAIRS * MLA_QK_W), kv_out(BRANCH_W), kv_out(LANES), kv_out(LANES),
            kv_out(LANES), kv_out(LANES),
            q_out(N_PAIRS * MLA_QK_W), q_out(BRANCH_W), q_out(BRANCH_W), q_out(Z_W), q_out(G_W)]
    return pl.pallas_call(
        functools.partial(_proj_kernel, update_ctx=update_ctx),
        grid=(B, nt),
        in_specs=[
            pl.BlockSpec((1, TOK_TILE, D), x_map),
            pl.BlockSpec((1, CTX_LEN, D), lambda b, j: (b, 0, 0)),
            pl.BlockSpec((1, 1, 3 * D), lambda b, j: (b, 0, 0)),
            _const_spec((1, 3 * D)),
            _const_spec((1, D)),
            pl.BlockSpec((TOK_TILE, TAB_W), lambda b, j: (j, 0)),
            _const_spec(wkv.shape), _const_spec(wq.shape),
            _const_spec(wuq.shape), _const_spec(wukv.shape),
            _const_spec(qn.shape), _const_spec(kvn.shape),
            _const_spec(axq.shape), _const_spec(axk.shape),
        ],
        out_specs=[o[0] for o in outs],
        out_shape=[o[1] for o in outs],
        scratch_shapes=[pltpu.VMEM((TOK_TILE, D), _BF16)],
        compiler_params=_params(2),
        name="project_ctx" if update_ctx else "project",
    )(x, ctx, modx, modc, norm_w, tab, wkv, wq, wuq, wukv, qn, kvn, axq, axk)


def _softmax_pv(s, v, sink=None):
    m = jnp.max(s, axis=-1, keepdims=True)
    if sink is not None:
        m = jnp.maximum(m, sink)
    p = jnp.exp(s - m)
    l = jnp.sum(p, axis=-1, keepdims=True)
    if sink is not None:
        l = l + jnp.exp(sink - m)
    r = jnp.dot(p.astype(_BF16), v, preferred_element_type=_F32)
    return r / l


def _scores(q, k):
    return lax.dot_general(q, k, (((1,), (1,)), ((), ())), preferred_element_type=_F32)


def _split_heads(q, mla):
    lane = lax.broadcasted_iota(jnp.int32, q.shape, 1)
    is_a = lane < HEAD_DIM
    if mla:
        is_a = is_a | ((lane >= LANES) & (lane < LANES + MLA_ROPE))
    zero = jnp.zeros_like(q)
    return jnp.where(is_a, q, zero), jnp.where(is_a, zero, q)


def _gate_store(o_ref, z_ref, r_a, r_b):
    lane = lax.broadcasted_iota(jnp.int32, r_a.shape, 1)
    o = jnp.where(lane < HEAD_DIM, r_a, r_b)
    o_ref[0] = (o * z_ref[0].astype(_F32)).astype(_BF16)


def _dense_kernel(q_ref, k_ref, v_ref, z_ref, o_ref, *, mla, has_ctx_q):
    i = pl.program_id(2)

    def run(nk):
        q_a, q_b = _split_heads(q_ref[0], mla)
        k = k_ref[0, :nk]
        v = v_ref[0, :nk]
        r_a = _softmax_pv(_scores(q_a, k), v)
        r_b = _softmax_pv(_scores(q_b, k), v)
        _gate_store(o_ref, z_ref, r_a, r_b)

    n_tok = k_ref.shape[1]
    if has_ctx_q:
        pl.when(i == 0)(lambda: run(CTX_LEN))
        pl.when(i > 0)(lambda: run(n_tok))
    else:
        run(n_tok)


def _dense_attention(q, k, v, z, z_blk0, *, mla, has_ctx_q):
    B, tq_total, _ = q.shape
    n_tok = k.shape[1]
    wq = MLA_QK_W if mla else LANES
    kv_map = (lambda b, p, i: (b, 0, p)) if mla else (lambda b, p, i: (b, 0, 0))
    return pl.pallas_call(
        functools.partial(_dense_kernel, mla=mla, has_ctx_q=has_ctx_q),
        grid=(B, N_PAIRS, tq_total // Q_TILE),
        in_specs=[
            pl.BlockSpec((1, Q_TILE, wq), lambda b, p, i: (b, i, p)),
            pl.BlockSpec((1, n_tok, wq), kv_map),
            pl.BlockSpec((1, n_tok, LANES), kv_map),
            pl.BlockSpec((1, Q_TILE, LANES), lambda b, p, i: (b, i, z_blk0 + p)),
        ],
        out_specs=pl.BlockSpec((1, Q_TILE, LANES), lambda b, p, i: (b, i, p)),
        out_shape=jax.ShapeDtypeStruct((B, tq_total, BRANCH_W), _BF16),
        compiler_params=_params(3),
        name="attn_mla" if mla else "attn_axial",
    )(q, k, v, z)


def _window_kernel(sink_ref, q_ref, k_ref, v_ref, z_ref, o_ref, kcat, vcat, *, has_ctx_q):
    p = pl.program_id(1)
    i = pl.program_id(2)
    sink_a = sink_ref[p]
    sink_b = sink_ref[N_PAIRS + p]

    def ctx_tile():
        q_a, q_b = _split_heads(q_ref[0], False)
        k = k_ref[0, :CTX_LEN]
        v = v_ref[0, :CTX_LEN]
        r_a = _softmax_pv(_scores(q_a, k), v, sink_a)
        r_b = _softmax_pv(_scores(q_b, k), v, sink_b)
        _gate_store(o_ref, z_ref, r_a, r_b)

    def latent_tile(ix):
        n_lat = k_ref.shape[1] - CTX_LEN
        q0 = ix * Q_TILE
        k0 = jnp.clip(q0 - WINDOW, 0, n_lat - BAND)
        start = pl.multiple_of(CTX_LEN + k0, LANES)
        kcat[:CTX_LEN] = k_ref[0, :CTX_LEN]
        vcat[:CTX_LEN] = v_ref[0, :CTX_LEN]
        kcat[CTX_LEN:] = k_ref[0, pl.ds(start, BAND)]
        vcat[CTX_LEN:] = v_ref[0, pl.ds(start, BAND)]
        k = kcat[...]
        v = vcat[...]
        shape = (Q_TILE, CTX_LEN + BAND)
        col = lax.broadcasted_iota(jnp.int32, shape, 1)
        row = lax.broadcasted_iota(jnp.int32, shape, 0)
        rel = (col - CTX_LEN + k0) - (row + q0)
        valid = (col < CTX_LEN) | (jnp.abs(rel) <= WINDOW)
        q_a, q_b = _split_heads(q_ref[0], False)
        s_a = jnp.where(valid, _scores(q_a, k), NEG_INF)
        s_b = jnp.where(valid, _scores(q_b, k), NEG_INF)
        r_a = _softmax_pv(s_a, v, sink_a)
        r_b = _softmax_pv(s_b, v, sink_b)
        _gate_store(o_ref, z_ref, r_a, r_b)

    if has_ctx_q:
        pl.when(i == 0)(ctx_tile)
        pl.when(i > 0)(lambda: latent_tile(i - 1))
    else:
        latent_tile(i)


def _window_attention(sink, q, k, v, z, z_blk0, *, has_ctx_q):
    B, tq_total, _ = q.shape
    n_tok = k.shape[1]
    return pl.pallas_call(
        functools.partial(_window_kernel, has_ctx_q=has_ctx_q),
        grid=(B, N_PAIRS, tq_total // Q_TILE),
        in_specs=[
            pl.BlockSpec(memory_space=pltpu.SMEM),
            pl.BlockSpec((1, Q_TILE, LANES), lambda b, p, i: (b, i, p)),
            pl.BlockSpec((1, n_tok, LANES), lambda b, p, i: (b, 0, 0)),
            pl.BlockSpec((1, n_tok, LANES), lambda b, p, i: (b, 0, 0)),
            pl.BlockSpec((1, Q_TILE, LANES), lambda b, p, i: (b, i, z_blk0 + p)),
        ],
        out_specs=pl.BlockSpec((1, Q_TILE, LANES), lambda b, p, i: (b, i, p)),
        out_shape=jax.ShapeDtypeStruct((B, tq_total, BRANCH_W), _BF16),
        scratch_shapes=[pltpu.VMEM((CTX_LEN + BAND, LANES), _BF16),
                        pltpu.VMEM((CTX_LEN + BAND, LANES), _BF16)],
        compiler_params=_params(3),
        name="attn_window",
    )(sink, q, k, v, z)


def _merge_kernel(um_ref, us_ref, ua_ref, sg_ref, wom_ref, wos_ref, woa_ref, wout_ref,
                  x_ref, ctx_ref, modx_ref, modc_ref, fnw_ref, *out_refs, update_ctx):
    def branch_sum():
        y = None
        for n, (u_ref, w_ref) in enumerate(((um_ref, wom_ref), (us_ref, wos_ref), (ua_ref, woa_ref))):
            yb = jnp.dot(u_ref[0], w_ref[...], preferred_element_type=_F32)
            yb = yb * sg_ref[0, :, n * D_MODEL:(n + 1) * D_MODEL].astype(_F32)
            y = yb if y is None else y + yb
        return jnp.dot(y.astype(_BF16), wout_ref[...], preferred_element_type=_F32)

    if update_ctx:
        xo_ref, co_ref = out_refs
        j = pl.program_id(1)

        @pl.when(j == 0)
        def _():
            co_ref[0] = ctx_ref[0] + modc_ref[:, 2 * D_MODEL:] * branch_sum()

        @pl.when(j > 0)
        def _():
            xo_ref[0] = x_ref[0] + modx_ref[0][:, 2 * D_MODEL:] * branch_sum()
    else:
        (xo_ref,) = out_refs
        xn = x_ref[0] + modx_ref[0][:, 2 * D_MODEL:] * branch_sum()
        xo_ref[0] = _rms(xn, fnw_ref[...])


def _merge(um, us, ua, sg, wom, wos, woa, wout, x, ctx, modx, modc, fnw, *, update_ctx):
    B, T, D = x.shape
    tq_total = um.shape[1]
    if update_ctx:
        x_map = lambda b, j: (b, jnp.maximum(j - 1, 0), 0)
    else:
        x_map = lambda b, j: (b, j, 0)
    tok = lambda w: pl.BlockSpec((1, TOK_TILE, w), lambda b, j: (b, j, 0))
    out_specs = [pl.BlockSpec((1, TOK_TILE, D), x_map)]
    out_shape = [jax.ShapeDtypeStruct((B, T, D), _F32)]
    if update_ctx:
        out_specs.append(pl.BlockSpec((1, CTX_LEN, D), lambda b, j: (b, 0, 0)))
        out_shape.append(jax.ShapeDtypeStruct(ctx.shape, _F32))
    return pl.pallas_call(
        functools.partial(_merge_kernel, update_ctx=update_ctx),
        grid=(B, tq_total // TOK_TILE),
        in_specs=[
            tok(BRANCH_W), tok(BRANCH_W), tok(BRANCH_W), tok(G_W),
            _const_spec(wom.shape), _const_spec(wos.shape), _const_spec(woa.shape),
            _const_spec(wout.shape),
            pl.BlockSpec((1, TOK_TILE, D), x_map),
            pl.BlockSpec((1, CTX_LEN, D), lambda b, j: (b, 0, 0)),
            pl.BlockSpec((1, 1, 3 * D), lambda b, j: (b, 0, 0)),
            _const_spec((1, 3 * D)),
            _const_spec((1, D)),
        ],
        out_specs=out_specs,
        out_shape=out_shape,
        compiler_params=_params(2),
        name="merge_ctx" if update_ctx else "merge_final",
    )(um, us, ua, sg, wom, wos, woa, wout, x, ctx, modx, modc, fnw)


def _axial_tables(d, n_lat):
    h = d // 2
    t = jnp.arange(n_lat, dtype=jnp.int32)
    pos_row = (t // GRID_W).astype(_F32)
    pos_col = (t % GRID_W).astype(_F32)
    freqs = ROPE_THETA ** (-jnp.arange(0, h, 2, dtype=_F32) / h)
    ang_r = pos_row[:, None] * freqs[None, :]
    ang_c = pos_col[:, None] * freqs[None, :]
    cr, sr, cc, sc = jnp.cos(ang_r), jnp.sin(ang_r), jnp.cos(ang_c), jnp.sin(ang_c)
    z = jnp.zeros_like(sr)
    cos = jnp.concatenate([cr, cr, cc, cc], axis=-1)
    sinm = jnp.concatenate([-sr, z, -sc, z], axis=-1)
    sinp = jnp.concatenate([z, sr, z, sc], axis=-1)
    return cos, sinm, sinp


def _rope_tables(n_lat):
    def with_ctx(t, fill):
        return jnp.concatenate([jnp.full((CTX_LEN, t.shape[1]), fill, _F32), t], axis=0)

    c64, m64, p64 = _axial_tables(HEAD_DIM, n_lat)
    c32, m32, p32 = _axial_tables(MLA_ROPE, n_lat)
    set64 = [with_ctx(jnp.tile(t, (1, 2)), f) for t, f in ((c64, 1.0), (m64, 0.0), (p64, 0.0))]
    pad = LANES - 2 * MLA_ROPE
    set32 = [with_ctx(jnp.concatenate([t, t, jnp.full((n_lat, pad), f, _F32)], axis=1), f)
             for t, f in ((c32, 1.0), (m32, 0.0), (p32, 0.0))]
    sets = (set64 + [t * GQA_SCALE for t in set64] + set32 + [t * MLA_SCALE for t in set32])
    return jnp.concatenate(sets, axis=1)


def _permute_heads(w, axis):
    shape = w.shape
    w = w.reshape(shape[:axis] + (KV_HEADS, HEADS // KV_HEADS, HEAD_DIM) + shape[axis + 1:])
    return jnp.swapaxes(w, axis, axis + 1).reshape(shape)


def _layer_weights(w_in, w_uq, w_ukv, w_o_swa, w_o_ax):
    D = w_in.shape[0]
    offs = {}
    o = 0
    for name, w in (("ckv", MLA_KV_RANK), ("kr", MLA_ROPE), ("sk", LANES), ("sv", LANES),
                    ("ak", LANES), ("av", LANES), ("cq", MLA_Q_RANK), ("sq", BRANCH_W),
                    ("aq", BRANCH_W), ("zm", BRANCH_W), ("zs", BRANCH_W), ("za", BRANCH_W),
                    ("g", G_W)):
        offs[name] = w_in[:, o:o + w]
        o += w
    kr = offs["kr"]
    kr2 = jnp.concatenate([kr, kr, jnp.zeros((D, LANES - 2 * MLA_ROPE), w_in.dtype)], axis=1)
    wkv = jnp.concatenate([offs["ckv"], kr2, offs["sk"], offs["sv"], offs["ak"], offs["av"]], axis=1)
    wq = jnp.concatenate([offs["cq"], _permute_heads(offs["sq"], 1), _permute_heads(offs["aq"], 1),
                          offs["zm"], _permute_heads(offs["zs"], 1), _permute_heads(offs["za"], 1),
                          offs["g"]], axis=1)
    uq = w_uq.reshape(MLA_Q_RANK, N_PAIRS, 2, HEAD_DIM + MLA_ROPE)
    nope = uq[..., :HEAD_DIM].reshape(MLA_Q_RANK, N_PAIRS, 2 * HEAD_DIM)
    rope = uq[..., HEAD_DIM:].reshape(MLA_Q_RANK, N_PAIRS, 2 * MLA_ROPE)
    pad = jnp.zeros((MLA_Q_RANK, N_PAIRS, MLA_QK_W - 2 * HEAD_DIM - 2 * MLA_ROPE), w_uq.dtype)
    wuq = jnp.concatenate([nope, rope, pad], axis=-1).reshape(MLA_Q_RANK, N_PAIRS * MLA_QK_W)
    ukv = w_ukv.reshape(MLA_KV_RANK, HEADS, 2, HEAD_DIM)
    wukv = jnp.concatenate([ukv[:, :, 0].reshape(MLA_KV_RANK, BRANCH_W),
                            ukv[:, :, 1].reshape(MLA_KV_RANK, BRANCH_W)], axis=1)
    bf = lambda a: a.astype(_BF16)
    return (bf(wkv), bf(wq), bf(wuq), bf(wukv),
            bf(_permute_heads(w_o_swa, 0)), bf(_permute_heads(w_o_ax, 0)))


def kernel(x, c, ctx, c_ctx, ada_w, ada_b, norm_w, w_in, mla_q_norm, mla_w_uq, mla_kv_norm, mla_w_ukv,
           swa_sink, ax_q_norm, ax_k_norm, w_o_mla, w_o_swa, w_o_ax, w_out, final_norm_w):
    B, T, D = x.shape
    depth = w_in.shape[0]
    assert (D, ctx.shape[1]) == (D_MODEL, CTX_LEN) and T % Q_TILE == 0 and T >= BAND

    mod_rows = -(-(B + 1) // 8) * 8
    cc = jnp.concatenate([c, c_ctx[None, :], jnp.zeros((mod_rows - B - 1, D), c.dtype)], axis=0)
    mod = _modulation(cc, ada_w, ada_b)
    tab = _rope_tables(T)
    fnw = final_norm_w.reshape(1, D)
    two_heads = lambda w: jnp.tile(w, 2).reshape(1, LANES)

    for l in range(depth):
        update_ctx = l < depth - 1
        wkv, wq, wuq, wukv, wos, woa = _layer_weights(w_in[l], mla_w_uq[l], mla_w_ukv[l], w_o_swa[l], w_o_ax[l])
        modx = mod[l, :B].reshape(B, 1, 3 * D)
        modc = mod[l, B:B + 1]
        km, vm, ks, vs, ka, va, qm, qs, qa, z, sg = _project(
            x, ctx, modx, modc, norm_w[l].reshape(1, D), tab, wkv, wq, wuq, wukv,
            mla_q_norm[l].reshape(1, -1), mla_kv_norm[l].reshape(1, -1),
            two_heads(ax_q_norm[l]), two_heads(ax_k_norm[l]), update_ctx=update_ctx)
        um = _dense_attention(qm, km, vm, z, 0, mla=True, has_ctx_q=update_ctx)
        us = _window_attention(swa_sink[l], qs, ks, vs, z, N_PAIRS, has_ctx_q=update_ctx)
        ua = _dense_attention(qa, ka, va, z, 2 * N_PAIRS, mla=False, has_ctx_q=update_ctx)
        outs = _merge(um, us, ua, sg, w_o_mla[l].astype(_BF16), wos, woa, w_out[l].astype(_BF16),
                      x, ctx, modx, modc, fnw, update_ctx=update_ctx)
        if update_ctx:
            x, ctx = outs
        else:
            (x,) = outs
    return x
```

```python
import functools

import jax
import jax.numpy as jnp
from jax import lax
from jax.experimental import pallas as pl
from jax.experimental.pallas import tpu as pltpu

D_MODEL = 1024
CTX_LEN = 256
GRID_W = 64
ROPE_THETA = 10000.0
RMS_EPS = 1e-6
NEG_INF = -1e30
N_BRANCH = 3
WINDOW = 128

HEADS = 8
KV_HEADS = 2
HEAD_DIM = 64
MLA_ROPE = 32
MLA_Q_RANK = 384
MLA_KV_RANK = 256
LOG2E = 1.4426950408889634
MLA_SCALE = (HEAD_DIM + MLA_ROPE) ** -0.5 * LOG2E
GQA_SCALE = HEAD_DIM ** -0.5 * LOG2E
BRANCH_W = HEADS * HEAD_DIM

LANES = 128
N_PAIRS = BRANCH_W // LANES
MLA_QK_W = 2 * LANES

TOK_TILE = 256
Q_TILE = 256
BAND = Q_TILE + 2 * WINDOW
KEY_TILE = 256

KV_W = MLA_KV_RANK + 5 * LANES
Q_W = MLA_Q_RANK + 5 * BRANCH_W + N_BRANCH * D_MODEL
Z_W = 3 * BRANCH_W
G_W = N_BRANCH * D_MODEL
TAB_W = 12 * LANES

VMEM_LIMIT = 56 * 1024 * 1024

_F32 = jnp.float32
_BF16 = jnp.bfloat16


def _params(n_axes):
    return pltpu.CompilerParams(
        dimension_semantics=("arbitrary",) * n_axes, vmem_limit_bytes=VMEM_LIMIT)


def _const_spec(shape):
    nd = len(shape)
    return pl.BlockSpec(shape, lambda *_: (0,) * nd, pipeline_mode=pl.Buffered(1))


def _mod_kernel(c_ref, w_ref, b_ref, o_ref):
    c = c_ref[...]
    s = c / (1.0 + jnp.exp(-c))
    o_ref[0] = jnp.dot(s, w_ref[0], preferred_element_type=_F32,
                       precision=lax.Precision.HIGHEST) + b_ref[0]


def _modulation(cc, ada_w, ada_b):
    depth = ada_w.shape[0]
    rows = cc.shape[0]
    col_tile = D_MODEL
    return pl.pallas_call(
        _mod_kernel,
        grid=(depth, (3 * D_MODEL) // col_tile),
        in_specs=[
            pl.BlockSpec((rows, D_MODEL), lambda l, n: (0, 0)),
            pl.BlockSpec((1, D_MODEL, col_tile), lambda l, n: (l, 0, n)),
            pl.BlockSpec((1, 1, col_tile), lambda l, n: (l, 0, n)),
        ],
        out_specs=pl.BlockSpec((1, rows, col_tile), lambda l, n: (l, 0, n)),
        out_shape=jax.ShapeDtypeStruct((depth, rows, 3 * D_MODEL), _F32),
        compiler_params=_params(2),
        name="modulation",
    )(cc, ada_w, ada_b.reshape(depth, 1, 3 * D_MODEL))


def _rms(x, w):
    ms = jnp.mean(x * x, axis=-1, keepdims=True)
    return x * lax.rsqrt(ms + RMS_EPS) * w


def _head_rms(x, w):
    lane = lax.broadcasted_iota(jnp.int32, x.shape, 1)
    y = x * x
    s = 1
    while s < HEAD_DIM:
        up = pltpu.roll(y, s, 1)
        dn = pltpu.roll(y, LANES - s, 1)
        y = y + jnp.where((lane & s) != 0, up, dn)
        s *= 2
    return x * lax.rsqrt(y * (1.0 / HEAD_DIM) + RMS_EPS) * w


def _rope(x, tab_ref, k, r):
    cos = tab_ref[:, (3 * k) * LANES:(3 * k + 1) * LANES]
    sinm = tab_ref[:, (3 * k + 1) * LANES:(3 * k + 2) * LANES]
    sinp = tab_ref[:, (3 * k + 2) * LANES:(3 * k + 3) * LANES]
    return x * cos + pltpu.roll(x, LANES - r, 1) * sinm + pltpu.roll(x, r, 1) * sinp


_TAB_K64, _TAB_Q64, _TAB_K32, _TAB_Q32 = 0, 1, 2, 3


def _proj_kernel(x_ref, ctx_ref, modx_ref, modc_ref, nw_ref, tab_ref,
                 wkv_ref, wq_ref, wuq_ref, wukv_ref, qn_ref, kvn_ref, axq_ref, axk_ref,
                 km_ref, vm_ref, ks_ref, vs_ref, ka_ref, va_ref,
                 qm_ref, qs_ref, qa_ref, z_ref, sg_ref, h_scr, *, update_ctx):
    j = pl.program_id(1)

    def norm_mod(xin, mod):
        shift = mod[:, :D_MODEL]
        scale = mod[:, D_MODEL:2 * D_MODEL]
        h = _rms(xin, nw_ref[...]) * (1.0 + scale) + shift
        h_scr[...] = h.astype(_BF16)

    @pl.when(j == 0)
    def _():
        norm_mod(ctx_ref[0], modc_ref[...])

    @pl.when(j > 0)
    def _():
        norm_mod(x_ref[0], modx_ref[0])

    h = h_scr[...]

    pkv = jnp.dot(h, wkv_ref[...], preferred_element_type=_F32)
    ckvn = _rms(pkv[:, :MLA_KV_RANK], kvn_ref[...]).astype(_BF16)
    kvm = jnp.dot(ckvn, wukv_ref[...], preferred_element_type=_F32)
    vm_ref[0] = kvm[:, BRANCH_W:].astype(_BF16)
    o = MLA_KV_RANK
    kr2 = _rope(pkv[:, o:o + LANES], tab_ref, _TAB_K32, MLA_ROPE // 4).astype(_BF16)
    for p in range(N_PAIRS):
        km_ref[0, :, p * MLA_QK_W:p * MLA_QK_W + LANES] = (
            kvm[:, p * LANES:(p + 1) * LANES].astype(_BF16))
        km_ref[0, :, p * MLA_QK_W + LANES:(p + 1) * MLA_QK_W] = kr2
    o += LANES
    ks_ref[0] = _rope(pkv[:, o:o + LANES], tab_ref, _TAB_K64, HEAD_DIM // 4).astype(_BF16)
    o += LANES
    vs_ref[0] = pkv[:, o:o + LANES].astype(_BF16)
    o += LANES
    ak = _head_rms(pkv[:, o:o + LANES], axk_ref[...])
    ka_ref[0] = _rope(ak, tab_ref, _TAB_K64, HEAD_DIM // 4).astype(_BF16)
    o += LANES
    va_ref[0] = pkv[:, o:o + LANES].astype(_BF16)

    def q_side():
        cq = jnp.dot(h, wq_ref[:, :MLA_Q_RANK], preferred_element_type=_F32)
        cqn = _rms(cq, qn_ref[...]).astype(_BF16)
        qm = jnp.dot(cqn, wuq_ref[...], preferred_element_type=_F32)
        for p in range(N_PAIRS):
            lo = p * MLA_QK_W
            qm_ref[0, :, lo:lo + LANES] = (qm[:, lo:lo + LANES] * MLA_SCALE).astype(_BF16)
            qm_ref[0, :, lo + LANES:lo + MLA_QK_W] = _rope(
                qm[:, lo + LANES:lo + MLA_QK_W], tab_ref, _TAB_Q32, MLA_ROPE // 4).astype(_BF16)
        c0 = MLA_Q_RANK
        sq = jnp.dot(h, wq_ref[:, c0:c0 + BRANCH_W], preferred_element_type=_F32)
        c0 += BRANCH_W
        aq = jnp.dot(h, wq_ref[:, c0:c0 + BRANCH_W], preferred_element_type=_F32)
        c0 += BRANCH_W
        for p in range(N_PAIRS):
            blk = slice(p * LANES, (p + 1) * LANES)
            qs_ref[0, :, blk] = _rope(sq[:, blk], tab_ref, _TAB_Q64, HEAD_DIM // 4).astype(_BF16)
            aqn = _head_rms(aq[:, blk], axq_ref[...])
            qa_ref[0, :, blk] = _rope(aqn, tab_ref, _TAB_Q64, HEAD_DIM // 4).astype(_BF16)
        for n in range(Z_W // BRANCH_W):
            z = jnp.dot(h, wq_ref[:, c0:c0 + BRANCH_W], preferred_element_type=_F32)
            z_ref[0, :, n * BRANCH_W:(n + 1) * BRANCH_W] = (z / (1.0 + jnp.exp(-z))).astype(_BF16)
            c0 += BRANCH_W
        for n in range(G_W // BRANCH_W):
            g = jnp.dot(h, wq_ref[:, c0:c0 + BRANCH_W], preferred_element_type=_F32)
            sg_ref[0, :, n * BRANCH_W:(n + 1) * BRANCH_W] = (1.0 / (1.0 + jnp.exp(-g))).astype(_BF16)
            c0 += BRANCH_W

    if update_ctx:
        q_side()
    else:
        pl.when(j > 0)(q_side)


def _project(x, ctx, modx, modc, norm_w, tab, wkv, wq, wuq, wukv, qn, kvn, axq, axk, *, update_ctx):
    B, T, D = x.shape
    n_tok = CTX_LEN + T
    nt = n_tok // TOK_TILE
    tq = n_tok if update_ctx else T

    def x_map(b, j):
        return (b, jnp.maximum(j - 1, 0), 0)

    q_map = (lambda b, j: (b, j, 0)) if update_ctx else x_map

    def kv_out(w):
        return (pl.BlockSpec((1, TOK_TILE, w), lambda b, j: (b, j, 0)),
                jax.ShapeDtypeStruct((B, n_tok, w), _BF16))

    def q_out(w):
        return (pl.BlockSpec((1, TOK_TILE, w), q_map),
                jax.ShapeDtypeStruct((B, tq, w), _BF16))

    outs = [kv_out(N_PAIRS * MLA_QK_W), kv_out(BRANCH_W), kv_out(LANES), kv_out(LANES),
            kv_out(LANES), kv_out(LANES),
            q_out(N_PAIRS * MLA_QK_W), q_out(BRANCH_W), q_out(BRANCH_W), q_out(Z_W), q_out(G_W)]
    return pl.pallas_call(
        functools.partial(_proj_kernel, update_ctx=update_ctx),
        grid=(B, nt),
        in_specs=[
            pl.BlockSpec((1, TOK_TILE, D), x_map),
            pl.BlockSpec((1, CTX_LEN, D), lambda b, j: (b, 0, 0)),
            pl.BlockSpec((1, 1, 3 * D), lambda b, j: (b, 0, 0)),
            _const_spec((1, 3 * D)),
            _const_spec((1, D)),
            pl.BlockSpec((TOK_TILE, TAB_W), lambda b, j: (j, 0)),
            _const_spec(wkv.shape), _const_spec(wq.shape),
            _const_spec(wuq.shape), _const_spec(wukv.shape),
            _const_spec(qn.shape), _const_spec(kvn.shape),
            _const_spec(axq.shape), _const_spec(axk.shape),
        ],
        out_specs=[o[0] for o in outs],
        out_shape=[o[1] for o in outs],
        scratch_shapes=[pltpu.VMEM((TOK_TILE, D), _BF16)],
        compiler_params=_params(2),
        name="project_ctx" if update_ctx else "project",
    )(x, ctx, modx, modc, norm_w, tab, wkv, wq, wuq, wukv, qn, kvn, axq, axk)


def _scores(q, k):
    return lax.dot_general(q, k, (((1,), (1,)), ((), ())), preferred_element_type=_F32)


def _stack_heads(q, mla):
    lane = lax.broadcasted_iota(jnp.int32, q.shape, 1)
    is_a = lane < HEAD_DIM
    if mla:
        is_a = is_a | ((lane >= LANES) & (lane < LANES + MLA_ROPE))
    zero = jnp.zeros_like(q)
    return jnp.concatenate([jnp.where(is_a, q, zero), jnp.where(is_a, zero, q)], axis=0)


def _fill_v1(v1_scr, v_ref, n_blocks):
    n_tok = v_ref.shape[1]
    for j in range(n_blocks):
        v1_scr[j, :, :LANES] = v_ref[0, :, j * LANES:(j + 1) * LANES]
        v1_scr[j, :, LANES:] = jnp.ones((n_tok, LANES), _BF16)


def _online_softmax_pv(q2, key_tiles, sink=None):
    m = acc = None
    if sink is not None:
        lane = lax.broadcasted_iota(jnp.int32, (q2.shape[0], 2 * LANES), 1)
        m, acc = sink, jnp.where(lane < LANES, 0.0, 1.0)
    for k, v1, valid in key_tiles:
        s = _scores(q2, k)
        if valid is not None:
            s = jnp.where(valid, s, NEG_INF)
        m_new = jnp.max(s, axis=-1, keepdims=True)
        if m is not None:
            m_new = jnp.maximum(m, m_new)
        pv = jnp.dot(jnp.exp2(s - m_new).astype(_BF16), v1, preferred_element_type=_F32)
        acc = pv if acc is None else jnp.exp2(m - m_new) * acc + pv
        m = m_new
    return acc[:, :LANES] / acc[:, LANES:]


def _gate_store(o_ref, z_ref, rows, p, r):
    n = r.shape[0] // 2
    lane = lax.broadcasted_iota(jnp.int32, (n, LANES), 1)
    o = jnp.where(lane < HEAD_DIM, r[:n], r[n:])
    blk = slice(p * LANES, (p + 1) * LANES)
    o_ref[0, rows, blk] = (o * z_ref[0, rows, blk].astype(_F32)).astype(_BF16)


def _tile_rows(first, tile):
    return pl.ds(pl.multiple_of(first + tile * Q_TILE, Q_TILE), Q_TILE)


def _dense_kernel(q_ref, k_ref, v_ref, z_ref, o_ref, v1_scr, *, mla, has_ctx_q):
    wq = MLA_QK_W if mla else LANES
    n_tok = k_ref.shape[1]
    q_off = CTX_LEN if has_ctx_q else 0
    n_tiles = (q_ref.shape[1] - q_off) // Q_TILE
    _fill_v1(v1_scr, v_ref, N_PAIRS if mla else 1)

    def unit(rows, p, n_keys):
        q2 = _stack_heads(q_ref[0, rows, p * wq:(p + 1) * wq], mla)
        kb = slice(p * wq, (p + 1) * wq) if mla else slice(None)
        tiles = [(k_ref[0, k0:k0 + KEY_TILE, kb], v1_scr[p if mla else 0, k0:k0 + KEY_TILE, :], None)
                 for k0 in range(0, n_keys, KEY_TILE)]
        _gate_store(o_ref, z_ref, rows, p, _online_softmax_pv(q2, tiles))

    if has_ctx_q:
        for p in range(N_PAIRS):
            unit(slice(0, CTX_LEN), p, CTX_LEN)

    def step(t, carry):
        for p in range(N_PAIRS):
            unit(_tile_rows(q_off, t), p, n_tok)
        return carry

    lax.fori_loop(0, n_tiles, step, 0)


def _attention_call(body, name, q, k, v, z, z_blk0, n_v1, extra_in=(), extra_specs=()):
    B, tq_total, _ = q.shape
    n_tok = k.shape[1]
    row = lambda w: pl.BlockSpec((1, tq_total, w), lambda b: (b, 0, 0))
    return pl.pallas_call(
        body,
        grid=(B,),
        in_specs=list(extra_specs) + [
            row(q.shape[2]),
            pl.BlockSpec((1, n_tok, k.shape[2]), lambda b: (b, 0, 0)),
            pl.BlockSpec((1, n_tok, v.shape[2]), lambda b: (b, 0, 0)),
            pl.BlockSpec((1, tq_total, BRANCH_W), lambda b: (b, 0, z_blk0)),
        ],
        out_specs=row(BRANCH_W),
        out_shape=jax.ShapeDtypeStruct((B, tq_total, BRANCH_W), _BF16),
        scratch_shapes=[pltpu.VMEM((n_v1, n_tok, 2 * LANES), _BF16)],
        compiler_params=_params(1),
        name=name,
    )(*extra_in, q, k, v, z)


def _dense_attention(q, k, v, z, z_blk0, *, mla, has_ctx_q):
    body = functools.partial(_dense_kernel, mla=mla, has_ctx_q=has_ctx_q)
    return _attention_call(body, "attn_mla" if mla else "attn_axial", q, k, v, z, z_blk0,
                           N_PAIRS if mla else 1)


def _window_kernel(sink_ref, q_ref, k_ref, v_ref, z_ref, o_ref, v1_scr, *, has_ctx_q):
    n_lat = k_ref.shape[1] - CTX_LEN
    q_off = CTX_LEN if has_ctx_q else 0
    n_tiles = (q_ref.shape[1] - q_off) // Q_TILE
    _fill_v1(v1_scr, v_ref, 1)

    def pair_sink(p):
        row = lax.broadcasted_iota(jnp.int32, (2 * Q_TILE, 1), 0)
        return jnp.where(row < Q_TILE, sink_ref[p], sink_ref[N_PAIRS + p]) * LOG2E

    def key_tile(start, valid=None):
        return (k_ref[0, pl.ds(start, KEY_TILE), :], v1_scr[0, pl.ds(start, KEY_TILE), :], valid)

    def unit(rows, p, tiles):
        q2 = _stack_heads(q_ref[0, rows, p * LANES:(p + 1) * LANES], False)
        _gate_store(o_ref, z_ref, rows, p, _online_softmax_pv(q2, tiles, pair_sink(p)))

    if has_ctx_q:
        for p in range(N_PAIRS):
            unit(slice(0, CTX_LEN), p, [key_tile(0)])

    def step(t, carry):
        q0 = t * Q_TILE
        k0 = jnp.clip(q0 - WINDOW, 0, n_lat - BAND)
        shape = (2 * Q_TILE, KEY_TILE)
        col = lax.broadcasted_iota(jnp.int32, shape, 1)
        row = lax.broadcasted_iota(jnp.int32, shape, 0) & (Q_TILE - 1)
        tiles = [key_tile(0)]
        for b0 in range(0, BAND, KEY_TILE):
            rel = (col + (k0 + b0)) - (row + q0)
            start = pl.multiple_of(CTX_LEN + k0 + b0, LANES)
            tiles.append(key_tile(start, jnp.abs(rel) <= WINDOW))
        for p in range(N_PAIRS):
            unit(_tile_rows(q_off, t), p, tiles)
        return carry

    lax.fori_loop(0, n_tiles, step, 0)


def _window_attention(sink, q, k, v, z, z_blk0, *, has_ctx_q):
    body = functools.partial(_window_kernel, has_ctx_q=has_ctx_q)
    return _attention_call(body, "attn_window", q, k, v, z, z_blk0, 1,
                           extra_in=(sink,), extra_specs=(pl.BlockSpec(memory_space=pltpu.SMEM),))


def _merge_kernel(um_ref, us_ref, ua_ref, sg_ref, wom_ref, wos_ref, woa_ref, wout_ref,
                  x_ref, ctx_ref, modx_ref, modc_ref, fnw_ref, *out_refs, update_ctx):
    def branch_sum():
        y = None
        for n, (u_ref, w_ref) in enumerate(((um_ref, wom_ref), (us_ref, wos_ref), (ua_ref, woa_ref))):
            yb = jnp.dot(u_ref[0], w_ref[...], preferred_element_type=_F32)
            yb = yb * sg_ref[0, :, n * D_MODEL:(n + 1) * D_MODEL].astype(_F32)
            y = yb if y is None else y + yb
        return jnp.dot(y.astype(_BF16), wout_ref[...], preferred_element_type=_F32)

    if update_ctx:
        xo_ref, co_ref = out_refs
        j = pl.program_id(1)

        @pl.when(j == 0)
        def _():
            co_ref[0] = ctx_ref[0] + modc_ref[:, 2 * D_MODEL:] * branch_sum()

        @pl.when(j > 0)
        def _():
            xo_ref[0] = x_ref[0] + modx_ref[0][:, 2 * D_MODEL:] * branch_sum()
    else:
        (xo_ref,) = out_refs
        xn = x_ref[0] + modx_ref[0][:, 2 * D_MODEL:] * branch_sum()
        xo_ref[0] = _rms(xn, fnw_ref[...])


def _merge(um, us, ua, sg, wom, wos, woa, wout, x, ctx, modx, modc, fnw, *, update_ctx):
    B, T, D = x.shape
    tq_total = um.shape[1]
    if update_ctx:
        x_map = lambda b, j: (b, jnp.maximum(j - 1, 0), 0)
    else:
        x_map = lambda b, j: (b, j, 0)
    tok = lambda w: pl.BlockSpec((1, TOK_TILE, w), lambda b, j: (b, j, 0))
    out_specs = [pl.BlockSpec((1, TOK_TILE, D), x_map)]
    out_shape = [jax.ShapeDtypeStruct((B, T, D), _F32)]
    if update_ctx:
        out_specs.append(pl.BlockSpec((1, CTX_LEN, D), lambda b, j: (b, 0, 0)))
        out_shape.append(jax.ShapeDtypeStruct(ctx.shape, _F32))
    return pl.pallas_call(
        functools.partial(_merge_kernel, update_ctx=update_ctx),
        grid=(B, tq_total // TOK_TILE),
        in_specs=[
            tok(BRANCH_W), tok(BRANCH_W), tok(BRANCH_W), tok(G_W),
            _const_spec(wom.shape), _const_spec(wos.shape), _const_spec(woa.shape),
            _const_spec(wout.shape),
            pl.BlockSpec((1, TOK_TILE, D), x_map),
            pl.BlockSpec((1, CTX_LEN, D), lambda b, j: (b, 0, 0)),
            pl.BlockSpec((1, 1, 3 * D), lambda b, j: (b, 0, 0)),
            _const_spec((1, 3 * D)),
            _const_spec((1, D)),
        ],
        out_specs=out_specs,
        out_shape=out_shape,
        compiler_params=_params(2),
        name="merge_ctx" if update_ctx else "merge_final",
    )(um, us, ua, sg, wom, wos, woa, wout, x, ctx, modx, modc, fnw)


def _axial_tables(d, n_lat):
    h = d // 2
    t = jnp.arange(n_lat, dtype=jnp.int32)
    pos_row = (t // GRID_W).astype(_F32)
    pos_col = (t % GRID_W).astype(_F32)
    freqs = ROPE_THETA ** (-jnp.arange(0, h, 2, dtype=_F32) / h)
    ang_r = pos_row[:, None] * freqs[None, :]
    ang_c = pos_col[:, None] * freqs[None, :]
    cr, sr, cc, sc = jnp.cos(ang_r), jnp.sin(ang_r), jnp.cos(ang_c), jnp.sin(ang_c)
    z = jnp.zeros_like(sr)
    cos = jnp.concatenate([cr, cr, cc, cc], axis=-1)
    sinm = jnp.concatenate([-sr, z, -sc, z], axis=-1)
    sinp = jnp.concatenate([z, sr, z, sc], axis=-1)
    return cos, sinm, sinp


def _rope_tables(n_lat):
    def with_ctx(t, fill):
        return jnp.concatenate([jnp.full((CTX_LEN, t.shape[1]), fill, _F32), t], axis=0)

    c64, m64, p64 = _axial_tables(HEAD_DIM, n_lat)
    c32, m32, p32 = _axial_tables(MLA_ROPE, n_lat)
    set64 = [with_ctx(jnp.tile(t, (1, 2)), f) for t, f in ((c64, 1.0), (m64, 0.0), (p64, 0.0))]
    pad = LANES - 2 * MLA_ROPE
    set32 = [with_ctx(jnp.concatenate([t, t, jnp.full((n_lat, pad), f, _F32)], axis=1), f)
             for t, f in ((c32, 1.0), (m32, 0.0), (p32, 0.0))]
    sets = (set64 + [t * GQA_SCALE for t in set64] + set32 + [t * MLA_SCALE for t in set32])
    return jnp.concatenate(sets, axis=1)


def _permute_heads(w, axis):
    shape = w.shape
    w = w.reshape(shape[:axis] + (KV_HEADS, HEADS // KV_HEADS, HEAD_DIM) + shape[axis + 1:])
    return jnp.swapaxes(w, axis, axis + 1).reshape(shape)


def _layer_weights(w_in, w_uq, w_ukv, w_o_swa, w_o_ax):
    D = w_in.shape[0]
    offs = {}
    o = 0
    for name, w in (("ckv", MLA_KV_RANK), ("kr", MLA_ROPE), ("sk", LANES), ("sv", LANES),
                    ("ak", LANES), ("av", LANES), ("cq", MLA_Q_RANK), ("sq", BRANCH_W),
                    ("aq", BRANCH_W), ("zm", BRANCH_W), ("zs", BRANCH_W), ("za", BRANCH_W),
                    ("g", G_W)):
        offs[name] = w_in[:, o:o + w]
        o += w
    kr = offs["kr"]
    kr2 = jnp.concatenate([kr, kr, jnp.zeros((D, LANES - 2 * MLA_ROPE), w_in.dtype)], axis=1)
    wkv = jnp.concatenate([offs["ckv"], kr2, offs["sk"], offs["sv"], offs["ak"], offs["av"]], axis=1)
    wq = jnp.concatenate([offs["cq"], _permute_heads(offs["sq"], 1), _permute_heads(offs["aq"], 1),
                          offs["zm"], _permute_heads(offs["zs"], 1), _permute_heads(offs["za"], 1),
                          offs["g"]], axis=1)
    uq = w_uq.reshape(MLA_Q_RANK, N_PAIRS, 2, HEAD_DIM + MLA_ROPE)
    nope = uq[..., :HEAD_DIM].reshape(MLA_Q_RANK, N_PAIRS, 2 * HEAD_DIM)
    rope = uq[..., HEAD_DIM:].reshape(MLA_Q_RANK, N_PAIRS, 2 * MLA_ROPE)
    pad = jnp.zeros((MLA_Q_RANK, N_PAIRS, MLA_QK_W - 2 * HEAD_DIM - 2 * MLA_ROPE), w_uq.dtype)
    wuq = jnp.concatenate([nope, rope, pad], axis=-1).reshape(MLA_Q_RANK, N_PAIRS * MLA_QK_W)
    ukv = w_ukv.reshape(MLA_KV_RANK, HEADS, 2, HEAD_DIM)
    wukv = jnp.concatenate([ukv[:, :, 0].reshape(MLA_KV_RANK, BRANCH_W),
                            ukv[:, :, 1].reshape(MLA_KV_RANK, BRANCH_W)], axis=1)
    bf = lambda a: a.astype(_BF16)
    return (bf(wkv), bf(wq), bf(wuq), bf(wukv),
            bf(_permute_heads(w_o_swa, 0)), bf(_permute_heads(w_o_ax, 0)))


def kernel(x, c, ctx, c_ctx, ada_w, ada_b, norm_w, w_in, mla_q_norm, mla_w_uq, mla_kv_norm, mla_w_ukv,
           swa_sink, ax_q_norm, ax_k_norm, w_o_mla, w_o_swa, w_o_ax, w_out, final_norm_w):
    B, T, D = x.shape
    depth = w_in.shape[0]
    assert (D, ctx.shape[1]) == (D_MODEL, CTX_LEN) and T % Q_TILE == 0 and T >= BAND

    mod_rows = -(-(B + 1) // 8) * 8
    cc = jnp.concatenate([c, c_ctx[None, :], jnp.zeros((mod_rows - B - 1, D), c.dtype)], axis=0)
    mod = _modulation(cc, ada_w, ada_b)
    tab = _rope_tables(T)
    fnw = final_norm_w.reshape(1, D)
    two_heads = lambda w: jnp.tile(w, 2).reshape(1, LANES)

    for l in range(depth):
        update_ctx = l < depth - 1
        wkv, wq, wuq, wukv, wos, woa = _layer_weights(w_in[l], mla_w_uq[l], mla_w_ukv[l], w_o_swa[l], w_o_ax[l])
        modx = mod[l, :B].reshape(B, 1, 3 * D)
        modc = mod[l, B:B + 1]
        km, vm, ks, vs, ka, va, qm, qs, qa, z, sg = _project(
            x, ctx, modx, modc, norm_w[l].reshape(1, D), tab, wkv, wq, wuq, wukv,
            mla_q_norm[l].reshape(1, -1), mla_kv_norm[l].reshape(1, -1),
            two_heads(ax_q_norm[l]), two_heads(ax_k_norm[l]), update_ctx=update_ctx)
        um = _dense_attention(qm, km, vm, z, 0, mla=True, has_ctx_q=update_ctx)
        us = _window_attention(swa_sink[l], qs, ks, vs, z, 1, has_ctx_q=update_ctx)
        ua = _dense_attention(qa, ka, va, z, 2, mla=False, has_ctx_q=update_ctx)
        outs = _merge(um, us, ua, sg, w_o_mla[l].astype(_BF16), wos, woa, w_out[l].astype(_BF16),
                      x, ctx, modx, modc, fnw, update_ctx=update_ctx)
        if update_ctx:
            x, ctx = outs
        else:
            (x,) = outs
    return x
```

```python
import functools

import jax
import jax.numpy as jnp
from jax import lax
from jax.experimental import pallas as pl
from jax.experimental.pallas import tpu as pltpu

D_MODEL = 1024
CTX_LEN = 256
GRID_W = 64
ROPE_THETA = 10000.0
RMS_EPS = 1e-6
NEG_INF = -1e30
N_BRANCH = 3
WINDOW = 128

HEADS = 8
KV_HEADS = 2
HEAD_DIM = 64
MLA_ROPE = 32
MLA_Q_RANK = 384
MLA_KV_RANK = 256
LOG2E = 1.4426950408889634
MLA_SCALE = (HEAD_DIM + MLA_ROPE) ** -0.5 * LOG2E
GQA_SCALE = HEAD_DIM ** -0.5 * LOG2E
BRANCH_W = HEADS * HEAD_DIM

LANES = 128
N_PAIRS = BRANCH_W // LANES
MLA_QK_W = 2 * LANES

TOK_TILE = 512
Q_TILE = 256
BAND = Q_TILE + 2 * WINDOW
KEY_TILE = 256

KV_W = MLA_KV_RANK + 5 * LANES
Z_W = 3 * BRANCH_W
G_W = N_BRANCH * D_MODEL
TAB_W = 6 * LANES

VMEM_LIMIT = 56 * 1024 * 1024

_F32 = jnp.float32
_BF16 = jnp.bfloat16


def _params(n_axes):
    return pltpu.CompilerParams(
        dimension_semantics=("arbitrary",) * n_axes, vmem_limit_bytes=VMEM_LIMIT)


def _const_spec(shape):
    nd = len(shape)
    return pl.BlockSpec(shape, lambda *_: (0,) * nd, pipeline_mode=pl.Buffered(1))


def _mod_kernel(c_ref, w_ref, b_ref, o_ref):
    c = c_ref[...]
    s = c / (1.0 + jnp.exp(-c))
    o_ref[0] = jnp.dot(s, w_ref[0], preferred_element_type=_F32,
                       precision=lax.Precision.HIGHEST) + b_ref[0]


def _modulation(cc, ada_w, ada_b):
    depth = ada_w.shape[0]
    rows = cc.shape[0]
    col_tile = D_MODEL
    return pl.pallas_call(
        _mod_kernel,
        grid=(depth, (3 * D_MODEL) // col_tile),
        in_specs=[
            pl.BlockSpec((rows, D_MODEL), lambda l, n: (0, 0)),
            pl.BlockSpec((1, D_MODEL, col_tile), lambda l, n: (l, 0, n)),
            pl.BlockSpec((1, 1, col_tile), lambda l, n: (l, 0, n)),
        ],
        out_specs=pl.BlockSpec((1, rows, col_tile), lambda l, n: (l, 0, n)),
        out_shape=jax.ShapeDtypeStruct((depth, rows, 3 * D_MODEL), _F32),
        compiler_params=_params(2),
        name="modulation",
    )(cc, ada_w, ada_b.reshape(depth, 1, 3 * D_MODEL))


def _rms(x, w):
    ms = jnp.mean(x * x, axis=-1, keepdims=True)
    return x * lax.rsqrt(ms + RMS_EPS) * w


def _head_rms(x, w):
    lane = lax.broadcasted_iota(jnp.int32, x.shape, 1)
    y = x * x
    s = 1
    while s < HEAD_DIM:
        up = pltpu.roll(y, s, 1)
        dn = pltpu.roll(y, LANES - s, 1)
        y = y + jnp.where((lane & s) != 0, up, dn)
        s *= 2
    return x * lax.rsqrt(y * (1.0 / HEAD_DIM) + RMS_EPS) * w


def _rope(x, tab_ref, wide):
    if tab_ref is None:
        return x
    k, r = (0, HEAD_DIM // 4) if wide else (1, MLA_ROPE // 4)
    cos = tab_ref[:, (3 * k) * LANES:(3 * k + 1) * LANES]
    sinm = tab_ref[:, (3 * k + 1) * LANES:(3 * k + 2) * LANES]
    sinp = tab_ref[:, (3 * k + 2) * LANES:(3 * k + 3) * LANES]
    return x * cos + pltpu.roll(x, LANES - r, 1) * sinm + pltpu.roll(x, r, 1) * sinp


def _proj_kernel(*refs, has_rope, with_q):
    refs = list(refs)
    x_ref, mod_ref, nw_ref = refs[:3]
    del refs[:3]
    tab_ref = refs.pop(0) if has_rope else None
    wkv_ref, wq_ref, wuq_ref, wukv_ref, qn_ref, kvn_ref, axq_ref, axk_ref = refs[:8]
    km_ref, vm_ref, ks_ref, vs_ref, ka_ref, va_ref = refs[8:14]

    mod = mod_ref[0]
    h = (_rms(x_ref[0], nw_ref[...]) * (1.0 + mod[:, D_MODEL:2 * D_MODEL]) + mod[:, :D_MODEL]).astype(_BF16)

    pkv = jnp.dot(h, wkv_ref[...], preferred_element_type=_F32)
    ckvn = _rms(pkv[:, :MLA_KV_RANK], kvn_ref[...]).astype(_BF16)
    kvm = jnp.dot(ckvn, wukv_ref[...], preferred_element_type=_F32)
    vm_ref[0] = kvm[:, BRANCH_W:].astype(_BF16)
    o = MLA_KV_RANK
    kr2 = _rope(pkv[:, o:o + LANES], tab_ref, False).astype(_BF16)
    for p in range(N_PAIRS):
        km_ref[0, :, p * MLA_QK_W:p * MLA_QK_W + LANES] = (
            kvm[:, p * LANES:(p + 1) * LANES].astype(_BF16))
        km_ref[0, :, p * MLA_QK_W + LANES:(p + 1) * MLA_QK_W] = kr2
    o += LANES
    ks_ref[0] = _rope(pkv[:, o:o + LANES], tab_ref, True).astype(_BF16)
    o += LANES
    vs_ref[0] = pkv[:, o:o + LANES].astype(_BF16)
    o += LANES
    ka_ref[0] = _rope(_head_rms(pkv[:, o:o + LANES], axk_ref[...]), tab_ref, True).astype(_BF16)
    o += LANES
    va_ref[0] = pkv[:, o:o + LANES].astype(_BF16)
    if not with_q:
        return

    qm_ref, qs_ref, qa_ref, z_ref, sg_ref = refs[14:]
    cq = jnp.dot(h, wq_ref[:, :MLA_Q_RANK], preferred_element_type=_F32)
    cqn = _rms(cq, qn_ref[...]).astype(_BF16)
    qm = jnp.dot(cqn, wuq_ref[...], preferred_element_type=_F32)
    for p in range(N_PAIRS):
        lo = p * MLA_QK_W
        qm_ref[0, :, lo:lo + LANES] = (qm[:, lo:lo + LANES] * MLA_SCALE).astype(_BF16)
        qm_ref[0, :, lo + LANES:lo + MLA_QK_W] = (
            _rope(qm[:, lo + LANES:lo + MLA_QK_W], tab_ref, False) * MLA_SCALE).astype(_BF16)
    c0 = MLA_Q_RANK
    sq = jnp.dot(h, wq_ref[:, c0:c0 + BRANCH_W], preferred_element_type=_F32)
    c0 += BRANCH_W
    aq = jnp.dot(h, wq_ref[:, c0:c0 + BRANCH_W], preferred_element_type=_F32)
    c0 += BRANCH_W
    for p in range(N_PAIRS):
        blk = slice(p * LANES, (p + 1) * LANES)
        qs_ref[0, :, blk] = (_rope(sq[:, blk], tab_ref, True) * GQA_SCALE).astype(_BF16)
        aqn = _head_rms(aq[:, blk], axq_ref[...])
        qa_ref[0, :, blk] = (_rope(aqn, tab_ref, True) * GQA_SCALE).astype(_BF16)
    for n in range(Z_W // BRANCH_W):
        z = jnp.dot(h, wq_ref[:, c0:c0 + BRANCH_W], preferred_element_type=_F32)
        z_ref[0, :, n * BRANCH_W:(n + 1) * BRANCH_W] = (z / (1.0 + jnp.exp(-z))).astype(_BF16)
        c0 += BRANCH_W
    for n in range(G_W // BRANCH_W):
        g = jnp.dot(h, wq_ref[:, c0:c0 + BRANCH_W], preferred_element_type=_F32)
        sg_ref[0, :, n * BRANCH_W:(n + 1) * BRANCH_W] = (1.0 / (1.0 + jnp.exp(-g))).astype(_BF16)
        c0 += BRANCH_W


def _project(x, mod, norm_w, tab, weights, *, tile, with_q, name):
    B, N, D = x.shape
    mod_map = (lambda b, j: (b, 0, 0)) if mod.shape[0] == B else (lambda b, j: (0, 0, 0))
    widths = [N_PAIRS * MLA_QK_W, BRANCH_W, LANES, LANES, LANES, LANES]
    if with_q:
        widths += [N_PAIRS * MLA_QK_W, BRANCH_W, BRANCH_W, Z_W, G_W]
    in_specs = [pl.BlockSpec((1, tile, D), lambda b, j: (b, j, 0)),
                pl.BlockSpec((1, 1, 3 * D), mod_map),
                _const_spec(norm_w.shape)]
    args = [x, mod, norm_w]
    if tab is not None:
        in_specs.append(pl.BlockSpec((tile, TAB_W), lambda b, j: (j, 0)))
        args.append(tab)
    in_specs += [_const_spec(w.shape) for w in weights]
    return pl.pallas_call(
        functools.partial(_proj_kernel, has_rope=tab is not None, with_q=with_q),
        grid=(B, N // tile),
        in_specs=in_specs,
        out_specs=[pl.BlockSpec((1, tile, w), lambda b, j: (b, j, 0)) for w in widths],
        out_shape=[jax.ShapeDtypeStruct((B, N, w), _BF16) for w in widths],
        compiler_params=_params(2),
        name=name,
    )(*args, *weights)


def _scores(q, k):
    return lax.dot_general(q, k, (((1,), (1,)), ((), ())), preferred_element_type=_F32)


def _stack_heads(q, mla):
    lane = lax.broadcasted_iota(jnp.int32, q.shape, 1)
    is_a = lane < HEAD_DIM
    if mla:
        is_a = is_a | ((lane >= LANES) & (lane < LANES + MLA_ROPE))
    zero = jnp.zeros_like(q)
    return jnp.concatenate([jnp.where(is_a, q, zero), jnp.where(is_a, zero, q)], axis=0)


def _fill_v1(v1_scr, v_ref, n_blocks):
    n = v_ref.shape[1]
    for j in range(n_blocks):
        v1_scr[j, :, :LANES] = v_ref[0, :, j * LANES:(j + 1) * LANES]
        v1_scr[j, :, LANES:] = jnp.ones((n, LANES), _BF16)


def _online_softmax_pv(q2, key_tiles, sink=None):
    m = acc = None
    if sink is not None:
        lane = lax.broadcasted_iota(jnp.int32, (q2.shape[0], 2 * LANES), 1)
        m, acc = sink, jnp.where(lane < LANES, 0.0, 1.0)
    for k, v1, valid in key_tiles:
        s = _scores(q2, k)
        if valid is not None:
            s = jnp.where(valid, s, NEG_INF)
        m_new = jnp.max(s, axis=-1, keepdims=True)
        if m is not None:
            m_new = jnp.maximum(m, m_new)
        pv = jnp.dot(jnp.exp2(s - m_new).astype(_BF16), v1, preferred_element_type=_F32)
        acc = pv if acc is None else jnp.exp2(m - m_new) * acc + pv
        m = m_new
    return acc[:, :LANES] / acc[:, LANES:]


def _gate_store(o_ref, z_ref, rows, p, r):
    n = r.shape[0] // 2
    lane = lax.broadcasted_iota(jnp.int32, (n, LANES), 1)
    o = jnp.where(lane < HEAD_DIM, r[:n], r[n:])
    blk = slice(p * LANES, (p + 1) * LANES)
    o_ref[0, rows, blk] = (o * z_ref[0, rows, blk].astype(_F32)).astype(_BF16)


def _tile_rows(tile):
    return pl.ds(pl.multiple_of(tile * Q_TILE, Q_TILE), Q_TILE)


def _attn_kernel(*refs, mla, window, latent_keys):
    refs = list(refs)
    sink_ref = refs.pop(0) if window else None
    q_ref, z_ref, kc_ref, vc_ref = refs[:4]
    k_ref, v_ref = refs[4:6] if latent_keys else (None, None)
    o_ref, vc1_scr = refs[-2 - latent_keys], refs[-1 - latent_keys]
    v1_scr = refs[-1] if latent_keys else None
    wq = MLA_QK_W if mla else LANES
    n_blocks = N_PAIRS if mla else 1
    n_tiles = q_ref.shape[1] // Q_TILE
    _fill_v1(vc1_scr, vc_ref, n_blocks)
    if latent_keys:
        _fill_v1(v1_scr, v_ref, n_blocks)
        n_lat = k_ref.shape[1]

    def key_tile(kr, v1s, p, start, valid=None):
        kb = slice(p * wq, (p + 1) * wq) if mla else slice(None)
        return (kr[0, pl.ds(start, KEY_TILE), kb], v1s[p if mla else 0, pl.ds(start, KEY_TILE), :], valid)

    def pair_sink(p):
        row = lax.broadcasted_iota(jnp.int32, (2 * Q_TILE, 1), 0)
        return jnp.where(row < Q_TILE, sink_ref[p], sink_ref[N_PAIRS + p]) * LOG2E

    def step(t, carry):
        if window and latent_keys:
            q0 = t * Q_TILE
            k0 = jnp.clip(q0 - WINDOW, 0, n_lat - BAND)
            shape = (2 * Q_TILE, KEY_TILE)
            col = lax.broadcasted_iota(jnp.int32, shape, 1)
            row = lax.broadcasted_iota(jnp.int32, shape, 0) & (Q_TILE - 1)
            lat = [(pl.multiple_of(k0 + b0, LANES), jnp.abs((col + (k0 + b0)) - (row + q0)) <= WINDOW)
                   for b0 in range(0, BAND, KEY_TILE)]
        elif latent_keys:
            lat = [(k0, None) for k0 in range(0, n_lat, KEY_TILE)]
        else:
            lat = []
        rows = _tile_rows(t)
        for p in range(N_PAIRS):
            tiles = [key_tile(k_ref, v1_scr, p, start, valid) for start, valid in lat]
            tiles.append(key_tile(kc_ref, vc1_scr, p, 0))
            q2 = _stack_heads(q_ref[0, rows, p * wq:(p + 1) * wq], mla)
            r = _online_softmax_pv(q2, tiles, pair_sink(p) if window else None)
            _gate_store(o_ref, z_ref, rows, p, r)
        return carry

    lax.fori_loop(0, n_tiles, step, 0)


def _attention(q, z, z_blk, kc, vc, k=None, v=None, *, mla=False, sink=None, name):
    B, nq, _ = q.shape
    latent_keys = k is not None
    row = lambda a: pl.BlockSpec((1,) + a.shape[1:], lambda b: (b, 0, 0))
    args, in_specs = [], []
    if sink is not None:
        args.append(sink)
        in_specs.append(pl.BlockSpec(memory_space=pltpu.SMEM))
    args += [q, z, kc, vc]
    in_specs += [row(q), pl.BlockSpec((1, nq, BRANCH_W), lambda b: (b, 0, z_blk)), row(kc), row(vc)]
    n_blocks = N_PAIRS if mla else 1
    scratch = [pltpu.VMEM((n_blocks, CTX_LEN, 2 * LANES), _BF16)]
    if latent_keys:
        args += [k, v]
        in_specs += [row(k), row(v)]
        scratch.append(pltpu.VMEM((n_blocks, k.shape[1], 2 * LANES), _BF16))
    return pl.pallas_call(
        functools.partial(_attn_kernel, mla=mla, window=sink is not None, latent_keys=latent_keys),
        grid=(B,),
        in_specs=in_specs,
        out_specs=pl.BlockSpec((1, nq, BRANCH_W), lambda b: (b, 0, 0)),
        out_shape=jax.ShapeDtypeStruct((B, nq, BRANCH_W), _BF16),
        scratch_shapes=scratch,
        compiler_params=_params(1),
        name=name,
    )(*args)


def _merge_kernel(um_ref, us_ref, ua_ref, sg_ref, wom_ref, wos_ref, woa_ref, wout_ref,
                  x_ref, mod_ref, *rest, final_norm):
    y = None
    for n, (u_ref, w_ref) in enumerate(((um_ref, wom_ref), (us_ref, wos_ref), (ua_ref, woa_ref))):
        yb = jnp.dot(u_ref[0], w_ref[...], preferred_element_type=_F32)
        yb = yb * sg_ref[0, :, n * D_MODEL:(n + 1) * D_MODEL].astype(_F32)
        y = yb if y is None else y + yb
    out = jnp.dot(y.astype(_BF16), wout_ref[...], preferred_element_type=_F32)
    xn = x_ref[0] + mod_ref[0][:, 2 * D_MODEL:] * out
    if final_norm:
        fnw_ref, o_ref = rest
        o_ref[0] = _rms(xn, fnw_ref[...])
    else:
        (o_ref,) = rest
        o_ref[0] = xn


def _merge(us, sg, weights, x, mod, fnw=None, *, tile, name):
    B, N, D = x.shape
    mod_map = (lambda b, j: (b, 0, 0)) if mod.shape[0] == B else (lambda b, j: (0, 0, 0))
    tok = lambda w: pl.BlockSpec((1, tile, w), lambda b, j: (b, j, 0))
    in_specs = ([tok(BRANCH_W)] * 3 + [tok(G_W)] + [_const_spec(w.shape) for w in weights]
                + [tok(D), pl.BlockSpec((1, 1, 3 * D), mod_map)])
    args = [*us, sg, *weights, x, mod]
    if fnw is not None:
        in_specs.append(_const_spec(fnw.shape))
        args.append(fnw)
    return pl.pallas_call(
        functools.partial(_merge_kernel, final_norm=fnw is not None),
        grid=(B, N // tile),
        in_specs=in_specs,
        out_specs=tok(D),
        out_shape=jax.ShapeDtypeStruct((B, N, D), _F32),
        compiler_params=_params(2),
        name=name,
    )(*args)


def _axial_tables(d, n_lat):
    h = d // 2
    t = jnp.arange(n_lat, dtype=jnp.int32)
    pos_row = (t // GRID_W).astype(_F32)
    pos_col = (t % GRID_W).astype(_F32)
    freqs = ROPE_THETA ** (-jnp.arange(0, h, 2, dtype=_F32) / h)
    ang_r = pos_row[:, None] * freqs[None, :]
    ang_c = pos_col[:, None] * freqs[None, :]
    cr, sr, cc, sc = jnp.cos(ang_r), jnp.sin(ang_r), jnp.cos(ang_c), jnp.sin(ang_c)
    z = jnp.zeros_like(sr)
    cos = jnp.concatenate([cr, cr, cc, cc], axis=-1)
    sinm = jnp.concatenate([-sr, z, -sc, z], axis=-1)
    sinp = jnp.concatenate([z, sr, z, sc], axis=-1)
    return cos, sinm, sinp


def _rope_tables(n_lat):
    set64 = [jnp.tile(t, (1, 2)) for t in _axial_tables(HEAD_DIM, n_lat)]
    pad = LANES - 2 * MLA_ROPE
    set32 = [jnp.concatenate([t, t, jnp.full((n_lat, pad), f, _F32)], axis=1)
             for t, f in zip(_axial_tables(MLA_ROPE, n_lat), (1.0, 0.0, 0.0))]
    return jnp.concatenate(set64 + set32, axis=1)


def _permute_heads(w, axis):
    shape = w.shape
    w = w.reshape(shape[:axis] + (KV_HEADS, HEADS // KV_HEADS, HEAD_DIM) + shape[axis + 1:])
    return jnp.swapaxes(w, axis, axis + 1).reshape(shape)


def _layer_weights(w_in, w_uq, w_ukv, w_o_swa, w_o_ax):
    D = w_in.shape[0]
    offs = {}
    o = 0
    for name, w in (("ckv", MLA_KV_RANK), ("kr", MLA_ROPE), ("sk", LANES), ("sv", LANES),
                    ("ak", LANES), ("av", LANES), ("cq", MLA_Q_RANK), ("sq", BRANCH_W),
                    ("aq", BRANCH_W), ("zm", BRANCH_W), ("zs", BRANCH_W), ("za", BRANCH_W),
                    ("g", G_W)):
        offs[name] = w_in[:, o:o + w]
        o += w
    kr = offs["kr"]
    kr2 = jnp.concatenate([kr, kr, jnp.zeros((D, LANES - 2 * MLA_ROPE), w_in.dtype)], axis=1)
    wkv = jnp.concatenate([offs["ckv"], kr2, offs["sk"], offs["sv"], offs["ak"], offs["av"]], axis=1)
    wq = jnp.concatenate([offs["cq"], _permute_heads(offs["sq"], 1), _permute_heads(offs["aq"], 1),
                          offs["zm"], _permute_heads(offs["zs"], 1), _permute_heads(offs["za"], 1),
                          offs["g"]], axis=1)
    uq = w_uq.reshape(MLA_Q_RANK, N_PAIRS, 2, HEAD_DIM + MLA_ROPE)
    nope = uq[..., :HEAD_DIM].reshape(MLA_Q_RANK, N_PAIRS, 2 * HEAD_DIM)
    rope = uq[..., HEAD_DIM:].reshape(MLA_Q_RANK, N_PAIRS, 2 * MLA_ROPE)
    pad = jnp.zeros((MLA_Q_RANK, N_PAIRS, MLA_QK_W - 2 * HEAD_DIM - 2 * MLA_ROPE), w_uq.dtype)
    wuq = jnp.concatenate([nope, rope, pad], axis=-1).reshape(MLA_Q_RANK, N_PAIRS * MLA_QK_W)
    ukv = w_ukv.reshape(MLA_KV_RANK, HEADS, 2, HEAD_DIM)
    wukv = jnp.concatenate([ukv[:, :, 0].reshape(MLA_KV_RANK, BRANCH_W),
                            ukv[:, :, 1].reshape(MLA_KV_RANK, BRANCH_W)], axis=1)
    bf = lambda a: a.astype(_BF16)
    return (bf(wkv), bf(wq), bf(wuq), bf(wukv),
            bf(_permute_heads(w_o_swa, 0)), bf(_permute_heads(w_o_ax, 0)))


def _attend_all(sink, q3, z, kc3, vc3, k3=None, v3=None, *, tag):
    (qm, qs, qa), (kmc, ksc, kac), (vmc, vsc, vac) = q3, kc3, vc3
    km, ks, ka = k3 if k3 is not None else (None,) * 3
    vm, vs, va = v3 if v3 is not None else (None,) * 3
    return (_attention(qm, z, 0, kmc, vmc, km, vm, mla=True, name="attn_mla" + tag),
            _attention(qs, z, 1, ksc, vsc, ks, vs, sink=sink, name="attn_window" + tag),
            _attention(qa, z, 2, kac, vac, ka, va, name="attn_axial" + tag))


def kernel(x, c, ctx, c_ctx, ada_w, ada_b, norm_w, w_in, mla_q_norm, mla_w_uq, mla_kv_norm, mla_w_ukv,
           swa_sink, ax_q_norm, ax_k_norm, w_o_mla, w_o_swa, w_o_ax, w_out, final_norm_w):
    B, T, D = x.shape
    depth = w_in.shape[0]
    assert (D, ctx.shape[1]) == (D_MODEL, CTX_LEN) and T % TOK_TILE == 0 and T >= BAND

    mod_rows = -(-(B + 1) // 8) * 8
    cc = jnp.concatenate([c, c_ctx[None, :], jnp.zeros((mod_rows - B - 1, D), c.dtype)], axis=0)
    mod = _modulation(cc, ada_w, ada_b)
    tab = _rope_tables(T)
    fnw = final_norm_w.reshape(1, D)
    two_heads = lambda w: jnp.tile(w, 2).reshape(1, LANES)

    for l in range(depth):
        update_ctx = l < depth - 1
        wkv, wq, wuq, wukv, wos, woa = _layer_weights(w_in[l], mla_w_uq[l], mla_w_ukv[l], w_o_swa[l], w_o_ax[l])
        proj_w = (wkv, wq, wuq, wukv, mla_q_norm[l].reshape(1, -1), mla_kv_norm[l].reshape(1, -1),
                  two_heads(ax_q_norm[l]), two_heads(ax_k_norm[l]))
        merge_w = (w_o_mla[l].astype(_BF16), wos, woa, w_out[l].astype(_BF16))
        nw = norm_w[l].reshape(1, D)
        modx = mod[l, :B].reshape(B, 1, 3 * D)
        modc = mod[l, B:B + 1].reshape(1, 1, 3 * D)

        pc = _project(ctx, modc, nw, None, proj_w, tile=CTX_LEN, with_q=update_ctx, name="project_ctx")
        px = _project(x, modx, nw, tab, proj_w, tile=TOK_TILE, with_q=True, name="project")
        kc3, vc3 = pc[0:6:2], pc[1:6:2]
        k3, v3 = px[0:6:2], px[1:6:2]
        u3 = _attend_all(swa_sink[l], px[6:9], px[9], kc3, vc3, k3, v3, tag="")
        if update_ctx:
            uc3 = _attend_all(swa_sink[l], pc[6:9], pc[9], kc3, vc3, tag="_ctx")
            ctx = _merge(uc3, pc[10], merge_w, ctx, modc, tile=CTX_LEN, name="merge_ctx")
            x = _merge(u3, px[10], merge_w, x, modx, tile=TOK_TILE, name="merge")
        else:
            x = _merge(u3, px[10], merge_w, x, modx, fnw, tile=TOK_TILE, name="merge_final")
    return x
```

```python
import functools

import jax
import jax.numpy as jnp
from jax import lax
from jax.experimental import pallas as pl
from jax.experimental.pallas import tpu as pltpu

D_MODEL = 1024
CTX_LEN = 256
GRID_W = 64
ROPE_THETA = 10000.0
RMS_EPS = 1e-6
NEG_INF = -1e30
N_BRANCH = 3
WINDOW = 128

HEADS = 8
KV_HEADS = 2
HEAD_DIM = 64
MLA_ROPE = 32
MLA_Q_RANK = 384
MLA_KV_RANK = 256
LOG2E = 1.4426950408889634
MLA_SCALE = (HEAD_DIM + MLA_ROPE) ** -0.5 * LOG2E
GQA_SCALE = HEAD_DIM ** -0.5 * LOG2E
BRANCH_W = HEADS * HEAD_DIM

LANES = 128
N_PAIRS = BRANCH_W // LANES
MLA_QK_W = 2 * LANES

TOK_TILE = 512
Q_TILE = 256
BAND = Q_TILE + 2 * WINDOW
KEY_TILE = 256

KV_W = MLA_KV_RANK + 5 * LANES
Z_W = 3 * BRANCH_W
G_W = N_BRANCH * D_MODEL
TAB_W = 6 * LANES

VMEM_LIMIT = 56 * 1024 * 1024

_F32 = jnp.float32
_BF16 = jnp.bfloat16


def _params(n_axes):
    return pltpu.CompilerParams(
        dimension_semantics=("arbitrary",) * n_axes, vmem_limit_bytes=VMEM_LIMIT)


def _const_spec(shape):
    nd = len(shape)
    return pl.BlockSpec(shape, lambda *_: (0,) * nd, pipeline_mode=pl.Buffered(1))


def _mod_kernel(c_ref, w_ref, b_ref, o_ref):
    c = c_ref[...]
    s = c / (1.0 + jnp.exp(-c))
    o_ref[0] = jnp.dot(s, w_ref[0], preferred_element_type=_F32,
                       precision=lax.Precision.HIGHEST) + b_ref[0]


def _modulation(cc, ada_w, ada_b):
    depth = ada_w.shape[0]
    rows = cc.shape[0]
    col_tile = D_MODEL
    return pl.pallas_call(
        _mod_kernel,
        grid=(depth, (3 * D_MODEL) // col_tile),
        in_specs=[
            pl.BlockSpec((rows, D_MODEL), lambda l, n: (0, 0)),
            pl.BlockSpec((1, D_MODEL, col_tile), lambda l, n: (l, 0, n)),
            pl.BlockSpec((1, 1, col_tile), lambda l, n: (l, 0, n)),
        ],
        out_specs=pl.BlockSpec((1, rows, col_tile), lambda l, n: (l, 0, n)),
        out_shape=jax.ShapeDtypeStruct((depth, rows, 3 * D_MODEL), _F32),
        compiler_params=_params(2),
        name="modulation",
    )(cc, ada_w, ada_b.reshape(depth, 1, 3 * D_MODEL))


def _rms(x, w):
    ms = jnp.mean(x * x, axis=-1, keepdims=True)
    return x * lax.rsqrt(ms + RMS_EPS) * w


def _head_rms(x, w):
    lane = lax.broadcasted_iota(jnp.int32, x.shape, 1)
    y = x * x
    s = 1
    while s < HEAD_DIM:
        up = pltpu.roll(y, s, 1)
        dn = pltpu.roll(y, LANES - s, 1)
        y = y + jnp.where((lane & s) != 0, up, dn)
        s *= 2
    return x * lax.rsqrt(y * (1.0 / HEAD_DIM) + RMS_EPS) * w


def _rope(x, tab_ref, wide):
    if tab_ref is None:
        return x
    k, r = (0, HEAD_DIM // 4) if wide else (1, MLA_ROPE // 4)
    cos = tab_ref[:, (3 * k) * LANES:(3 * k + 1) * LANES]
    sinm = tab_ref[:, (3 * k + 1) * LANES:(3 * k + 2) * LANES]
    sinp = tab_ref[:, (3 * k + 2) * LANES:(3 * k + 3) * LANES]
    return x * cos + pltpu.roll(x, LANES - r, 1) * sinm + pltpu.roll(x, r, 1) * sinp


def _proj_kernel(*refs, has_rope, with_q):
    refs = list(refs)
    x_ref, mod_ref, nw_ref = refs[:3]
    del refs[:3]
    tab_ref = refs.pop(0) if has_rope else None
    wkv_ref, wq_ref, wuq_ref, wukv_ref, qn_ref, kvn_ref, axq_ref, axk_ref = refs[:8]
    km_ref, vm_ref, ks_ref, vs_ref, ka_ref, va_ref = refs[8:14]

    mod = mod_ref[0]
    x = x_ref[0]
    gain = nw_ref[...] * (1.0 + mod[:, D_MODEL:2 * D_MODEL])
    inv = lax.rsqrt(jnp.mean(x * x, axis=-1, keepdims=True) + RMS_EPS)
    h = (x * inv * gain + mod[:, :D_MODEL]).astype(_BF16)

    pkv = jnp.dot(h, wkv_ref[...], preferred_element_type=_F32)
    ckvn = _rms(pkv[:, :MLA_KV_RANK], kvn_ref[...]).astype(_BF16)
    kvm = jnp.dot(ckvn, wukv_ref[...], preferred_element_type=_F32)
    vm_ref[0] = kvm[:, BRANCH_W:].astype(_BF16)
    o = MLA_KV_RANK
    kr2 = _rope(pkv[:, o:o + LANES], tab_ref, False).astype(_BF16)
    for p in range(N_PAIRS):
        kn = kvm[:, p * LANES:(p + 1) * LANES]
        lo, hi = p * MLA_QK_W, p * MLA_QK_W + LANES
        if has_rope:
            km_ref[0, lo:hi, :] = kn.T.astype(_BF16)
            km_ref[0, hi:hi + LANES, :] = kr2.T
        else:
            km_ref[0, :, lo:hi] = kn.astype(_BF16)
            km_ref[0, :, hi:hi + LANES] = kr2
    o += LANES
    ks_ref[0] = _rope(pkv[:, o:o + LANES], tab_ref, True).astype(_BF16)
    o += LANES
    vs_ref[0] = pkv[:, o:o + LANES].astype(_BF16)
    o += LANES
    ka_ref[0] = _rope(_head_rms(pkv[:, o:o + LANES], axk_ref[...]), tab_ref, True).astype(_BF16)
    o += LANES
    va_ref[0] = pkv[:, o:o + LANES].astype(_BF16)
    if not with_q:
        return

    qm_ref, qs_ref, qa_ref, z_ref, sg_ref = refs[14:]
    cq = jnp.dot(h, wq_ref[:, :MLA_Q_RANK], preferred_element_type=_F32)
    cqn = _rms(cq, qn_ref[...]).astype(_BF16)
    qm = jnp.dot(cqn, wuq_ref[...], preferred_element_type=_F32)
    for p in range(N_PAIRS):
        lo = p * MLA_QK_W
        qm_ref[0, :, lo:lo + LANES] = (qm[:, lo:lo + LANES] * MLA_SCALE).astype(_BF16)
        qm_ref[0, :, lo + LANES:lo + MLA_QK_W] = (
            _rope(qm[:, lo + LANES:lo + MLA_QK_W], tab_ref, False) * MLA_SCALE).astype(_BF16)
    c0 = MLA_Q_RANK
    sq = jnp.dot(h, wq_ref[:, c0:c0 + BRANCH_W], preferred_element_type=_F32)
    c0 += BRANCH_W
    aq = jnp.dot(h, wq_ref[:, c0:c0 + BRANCH_W], preferred_element_type=_F32)
    c0 += BRANCH_W
    for p in range(N_PAIRS):
        blk = slice(p * LANES, (p + 1) * LANES)
        qs_ref[0, :, blk] = (_rope(sq[:, blk], tab_ref, True) * GQA_SCALE).astype(_BF16)
        aqn = _head_rms(aq[:, blk], axq_ref[...])
        qa_ref[0, :, blk] = (_rope(aqn, tab_ref, True) * GQA_SCALE).astype(_BF16)
    for n in range(Z_W // BRANCH_W):
        z = jnp.dot(h, wq_ref[:, c0:c0 + BRANCH_W], preferred_element_type=_F32)
        z_ref[0, :, n * BRANCH_W:(n + 1) * BRANCH_W] = (z / (1.0 + jnp.exp(-z))).astype(_BF16)
        c0 += BRANCH_W
    for n in range(G_W // BRANCH_W):
        g = jnp.dot(h, wq_ref[:, c0:c0 + BRANCH_W], preferred_element_type=_F32)
        sg_ref[0, :, n * BRANCH_W:(n + 1) * BRANCH_W] = (1.0 / (1.0 + jnp.exp(-g))).astype(_BF16)
        c0 += BRANCH_W


def _project(x, mod, norm_w, tab, weights, *, tile, with_q, name):
    B, N, D = x.shape
    mod_map = (lambda j, b: (b, 0, 0)) if mod.shape[0] == B else (lambda j, b: (0, 0, 0))
    widths = [N_PAIRS * MLA_QK_W, BRANCH_W, LANES, LANES, LANES, LANES]
    if with_q:
        widths += [N_PAIRS * MLA_QK_W, BRANCH_W, BRANCH_W, Z_W, G_W]
    in_specs = [pl.BlockSpec((1, tile, D), lambda j, b: (b, j, 0)),
                pl.BlockSpec((1, 1, 3 * D), mod_map),
                _const_spec(norm_w.shape)]
    args = [x, mod, norm_w]
    if tab is not None:
        in_specs.append(pl.BlockSpec((tile, TAB_W), lambda j, b: (j, 0)))
        args.append(tab)
    in_specs += [_const_spec(w.shape) for w in weights]
    out_specs = [pl.BlockSpec((1, tile, w), lambda j, b: (b, j, 0)) for w in widths]
    out_shape = [jax.ShapeDtypeStruct((B, N, w), _BF16) for w in widths]
    if tab is not None:
        out_specs[0] = pl.BlockSpec((1, widths[0], tile), lambda j, b: (b, 0, j))
        out_shape[0] = jax.ShapeDtypeStruct((B, widths[0], N), _BF16)
    return pl.pallas_call(
        functools.partial(_proj_kernel, has_rope=tab is not None, with_q=with_q),
        grid=(N // tile, B),
        in_specs=in_specs,
        out_specs=out_specs,
        out_shape=out_shape,
        compiler_params=_params(2),
        name=name,
    )(*args, *weights)


def _scores(q, k):
    return lax.dot_general(q, k, (((1,), (1,)), ((), ())), preferred_element_type=_F32)


def _stack_heads(q, mla):
    lane = lax.broadcasted_iota(jnp.int32, q.shape, 1)
    is_a = lane < HEAD_DIM
    if mla:
        is_a = is_a | ((lane >= LANES) & (lane < LANES + MLA_ROPE))
    zero = jnp.zeros_like(q)
    return jnp.concatenate([jnp.where(is_a, q, zero), jnp.where(is_a, zero, q)], axis=0)


def _fill_v1(v1_scr, v_ref, n_blocks):
    n = v_ref.shape[1]
    for j in range(n_blocks):
        v1_scr[j, :, :LANES] = v_ref[0, :, j * LANES:(j + 1) * LANES]
        v1_scr[j, :, LANES:] = jnp.ones((n, LANES), _BF16)


def _online_softmax_pv(q2, key_tiles, sink=None):
    m = acc = None
    if sink is not None:
        lane = lax.broadcasted_iota(jnp.int32, (q2.shape[0], 2 * LANES), 1)
        m, acc = sink, jnp.where(lane < LANES, 0.0, 1.0)
    for k, v1, bias, k_is_transposed in key_tiles:
        s = jnp.dot(q2, k, preferred_element_type=_F32) if k_is_transposed else _scores(q2, k)
        if bias is not None:
            s = s + bias
        m_new = jnp.max(s, axis=-1, keepdims=True)
        if m is not None:
            m_new = jnp.maximum(m, m_new)
        pv = jnp.dot(jnp.exp2(s - m_new).astype(_BF16), v1, preferred_element_type=_F32)
        acc = pv if acc is None else jnp.exp2(m - m_new) * acc + pv
        m = m_new
    return acc[:, :LANES] / acc[:, LANES:]


def _gate_store(o_ref, z_ref, rows, p, r):
    n = r.shape[0] // 2
    lane = lax.broadcasted_iota(jnp.int32, (n, LANES), 1)
    o = jnp.where(lane < HEAD_DIM, r[:n], r[n:])
    blk = slice(p * LANES, (p + 1) * LANES)
    o_ref[0, rows, blk] = (o * z_ref[0, rows, blk].astype(_F32)).astype(_BF16)


def _tile_rows(tile):
    return pl.ds(pl.multiple_of(tile * Q_TILE, Q_TILE), Q_TILE)


def _attn_kernel(*refs, mla, window, latent_keys):
    refs = list(refs)
    sink_ref = refs.pop(0) if window else None
    q_ref, z_ref, kc_ref, vc_ref = refs[:4]
    k_ref, v_ref = refs[4:6] if latent_keys else (None, None)
    o_ref, vc1_scr = refs[-2 - latent_keys], refs[-1 - latent_keys]
    v1_scr = refs[-1] if latent_keys else None
    wq = MLA_QK_W if mla else LANES
    n_blocks = N_PAIRS if mla else 1
    n_tiles = q_ref.shape[1] // Q_TILE
    _fill_v1(vc1_scr, vc_ref, n_blocks)
    if latent_keys:
        _fill_v1(v1_scr, v_ref, n_blocks)
        n_lat = v_ref.shape[1]

    def ctx_tile(p):
        kb = slice(p * wq, (p + 1) * wq) if mla else slice(None)
        return (kc_ref[0, :, kb], vc1_scr[p if mla else 0], None, False)

    def latent_tile(p, start, bias):
        keys = pl.ds(start, KEY_TILE)
        if mla:
            return (k_ref[0, p * wq:(p + 1) * wq, keys], v1_scr[p, keys, :], bias, True)
        return (k_ref[0, keys, :], v1_scr[0, keys, :], bias, False)

    def pair_sink(p):
        row = lax.broadcasted_iota(jnp.int32, (2 * Q_TILE, 1), 0)
        return jnp.where(row < Q_TILE, sink_ref[p], sink_ref[N_PAIRS + p]) * LOG2E

    def step(t, carry):
        if window and latent_keys:
            q0 = t * Q_TILE
            k0 = jnp.clip(q0 - WINDOW, 0, n_lat - BAND)
            shape = (2 * Q_TILE, KEY_TILE)
            col = lax.broadcasted_iota(jnp.int32, shape, 1)
            row = lax.broadcasted_iota(jnp.int32, shape, 0) & (Q_TILE - 1)
            lat = [(pl.multiple_of(k0 + b0, LANES),
                    jnp.where(jnp.abs((col + (k0 + b0)) - (row + q0)) <= WINDOW, 0.0, NEG_INF))
                   for b0 in range(0, BAND, KEY_TILE)]
        elif latent_keys:
            lat = [(k0, None) for k0 in range(0, n_lat, KEY_TILE)]
        else:
            lat = []
        rows = _tile_rows(t)
        for p in range(N_PAIRS):
            tiles = [latent_tile(p, start, bias) for start, bias in lat] + [ctx_tile(p)]
            q2 = _stack_heads(q_ref[0, rows, p * wq:(p + 1) * wq], mla)
            r = _online_softmax_pv(q2, tiles, pair_sink(p) if window else None)
            _gate_store(o_ref, z_ref, rows, p, r)
        return carry

    lax.fori_loop(0, n_tiles, step, 0)


def _attention(q, z, z_blk, kc, vc, k=None, v=None, *, mla=False, sink=None, name):
    B, nq, _ = q.shape
    latent_keys = k is not None
    row = lambda a: pl.BlockSpec((1,) + a.shape[1:], lambda b: (b, 0, 0))
    args, in_specs = [], []
    if sink is not None:
        args.append(sink)
        in_specs.append(pl.BlockSpec(memory_space=pltpu.SMEM))
    args += [q, z, kc, vc]
    in_specs += [row(q), pl.BlockSpec((1, nq, BRANCH_W), lambda b: (b, 0, z_blk)), row(kc), row(vc)]
    n_blocks = N_PAIRS if mla else 1
    scratch = [pltpu.VMEM((n_blocks, CTX_LEN, 2 * LANES), _BF16)]
    if latent_keys:
        args += [k, v]
        in_specs += [row(k), row(v)]
        scratch.append(pltpu.VMEM((n_blocks, v.shape[1], 2 * LANES), _BF16))
    return pl.pallas_call(
        functools.partial(_attn_kernel, mla=mla, window=sink is not None, latent_keys=latent_keys),
        grid=(B,),
        in_specs=in_specs,
        out_specs=pl.BlockSpec((1, nq, BRANCH_W), lambda b: (b, 0, 0)),
        out_shape=jax.ShapeDtypeStruct((B, nq, BRANCH_W), _BF16),
        scratch_shapes=scratch,
        compiler_params=_params(1),
        name=name,
    )(*args)


def _merge_kernel(um_ref, us_ref, ua_ref, sg_ref, wom_ref, wos_ref, woa_ref, wout_ref,
                  x_ref, mod_ref, *rest, final_norm):
    y = None
    for n, (u_ref, w_ref) in enumerate(((um_ref, wom_ref), (us_ref, wos_ref), (ua_ref, woa_ref))):
        yb = jnp.dot(u_ref[0], w_ref[...], preferred_element_type=_F32)
        yb = yb * sg_ref[0, :, n * D_MODEL:(n + 1) * D_MODEL].astype(_F32)
        y = yb if y is None else y + yb
    out = jnp.dot(y.astype(_BF16), wout_ref[...], preferred_element_type=_F32)
    xn = x_ref[0] + mod_ref[0][:, 2 * D_MODEL:] * out
    if final_norm:
        fnw_ref, o_ref = rest
        o_ref[0] = _rms(xn, fnw_ref[...])
    else:
        (o_ref,) = rest
        o_ref[0] = xn


def _merge(us, sg, weights, x, mod, fnw=None, *, tile, name):
    B, N, D = x.shape
    mod_map = (lambda b, j: (b, 0, 0)) if mod.shape[0] == B else (lambda b, j: (0, 0, 0))
    tok = lambda w: pl.BlockSpec((1, tile, w), lambda b, j: (b, j, 0))
    in_specs = ([tok(BRANCH_W)] * 3 + [tok(G_W)] + [_const_spec(w.shape) for w in weights]
                + [tok(D), pl.BlockSpec((1, 1, 3 * D), mod_map)])
    args = [*us, sg, *weights, x, mod]
    if fnw is not None:
        in_specs.append(_const_spec(fnw.shape))
        args.append(fnw)
    return pl.pallas_call(
        functools.partial(_merge_kernel, final_norm=fnw is not None),
        grid=(B, N // tile),
        in_specs=in_specs,
        out_specs=tok(D),
        out_shape=jax.ShapeDtypeStruct((B, N, D), _F32),
        compiler_params=_params(2),
        name=name,
    )(*args)


def _axial_tables(d, n_lat):
    h = d // 2
    t = jnp.arange(n_lat, dtype=jnp.int32)
    pos_row = (t // GRID_W).astype(_F32)
    pos_col = (t % GRID_W).astype(_F32)
    freqs = ROPE_THETA ** (-jnp.arange(0, h, 2, dtype=_F32) / h)
    ang_r = pos_row[:, None] * freqs[None, :]
    ang_c = pos_col[:, None] * freqs[None, :]
    cr, sr, cc, sc = jnp.cos(ang_r), jnp.sin(ang_r), jnp.cos(ang_c), jnp.sin(ang_c)
    z = jnp.zeros_like(sr)
    cos = jnp.concatenate([cr, cr, cc, cc], axis=-1)
    sinm = jnp.concatenate([-sr, z, -sc, z], axis=-1)
    sinp = jnp.concatenate([z, sr, z, sc], axis=-1)
    return cos, sinm, sinp


def _rope_tables(n_lat):
    set64 = [jnp.tile(t, (1, 2)) for t in _axial_tables(HEAD_DIM, n_lat)]
    pad = LANES - 2 * MLA_ROPE
    set32 = [jnp.concatenate([t, t, jnp.full((n_lat, pad), f, _F32)], axis=1)
             for t, f in zip(_axial_tables(MLA_ROPE, n_lat), (1.0, 0.0, 0.0))]
    return jnp.concatenate(set64 + set32, axis=1)


def _permute_heads(w, axis):
    shape = w.shape
    w = w.reshape(shape[:axis] + (KV_HEADS, HEADS // KV_HEADS, HEAD_DIM) + shape[axis + 1:])
    return jnp.swapaxes(w, axis, axis + 1).reshape(shape)


def _layer_weights(w_in, w_uq, w_ukv, w_o_swa, w_o_ax):
    D = w_in.shape[0]
    offs = {}
    o = 0
    for name, w in (("ckv", MLA_KV_RANK), ("kr", MLA_ROPE), ("sk", LANES), ("sv", LANES),
                    ("ak", LANES), ("av", LANES), ("cq", MLA_Q_RANK), ("sq", BRANCH_W),
                    ("aq", BRANCH_W), ("zm", BRANCH_W), ("zs", BRANCH_W), ("za", BRANCH_W),
                    ("g", G_W)):
        offs[name] = w_in[:, o:o + w]
        o += w
    kr = offs["kr"]
    kr2 = jnp.concatenate([kr, kr, jnp.zeros((D, LANES - 2 * MLA_ROPE), w_in.dtype)], axis=1)
    wkv = jnp.concatenate([offs["ckv"], kr2, offs["sk"], offs["sv"], offs["ak"], offs["av"]], axis=1)
    wq = jnp.concatenate([offs["cq"], _permute_heads(offs["sq"], 1), _permute_heads(offs["aq"], 1),
                          offs["zm"], _permute_heads(offs["zs"], 1), _permute_heads(offs["za"], 1),
                          offs["g"]], axis=1)
    uq = w_uq.reshape(MLA_Q_RANK, N_PAIRS, 2, HEAD_DIM + MLA_ROPE)
    nope = uq[..., :HEAD_DIM].reshape(MLA_Q_RANK, N_PAIRS, 2 * HEAD_DIM)
    rope = uq[..., HEAD_DIM:].reshape(MLA_Q_RANK, N_PAIRS, 2 * MLA_ROPE)
    pad = jnp.zeros((MLA_Q_RANK, N_PAIRS, MLA_QK_W - 2 * HEAD_DIM - 2 * MLA_ROPE), w_uq.dtype)
    wuq = jnp.concatenate([nope, rope, pad], axis=-1).reshape(MLA_Q_RANK, N_PAIRS * MLA_QK_W)
    ukv = w_ukv.reshape(MLA_KV_RANK, HEADS, 2, HEAD_DIM)
    wukv = jnp.concatenate([ukv[:, :, 0].reshape(MLA_KV_RANK, BRANCH_W),
                            ukv[:, :, 1].reshape(MLA_KV_RANK, BRANCH_W)], axis=1)
    bf = lambda a: a.astype(_BF16)
    return (bf(wkv), bf(wq), bf(wuq), bf(wukv),
            bf(_permute_heads(w_o_swa, 0)), bf(_permute_heads(w_o_ax, 0)))


def _attend_all(sink, q3, z, kc3, vc3, k3=None, v3=None, *, tag):
    (qm, qs, qa), (kmc, ksc, kac), (vmc, vsc, vac) = q3, kc3, vc3
    km, ks, ka = k3 if k3 is not None else (None,) * 3
    vm, vs, va = v3 if v3 is not None else (None,) * 3
    return (_attention(qm, z, 0, kmc, vmc, km, vm, mla=True, name="attn_mla" + tag),
            _attention(qs, z, 1, ksc, vsc, ks, vs, sink=sink, name="attn_window" + tag),
            _attention(qa, z, 2, kac, vac, ka, va, name="attn_axial" + tag))


def kernel(x, c, ctx, c_ctx, ada_w, ada_b, norm_w, w_in, mla_q_norm, mla_w_uq, mla_kv_norm, mla_w_ukv,
           swa_sink, ax_q_norm, ax_k_norm, w_o_mla, w_o_swa, w_o_ax, w_out, final_norm_w):
    B, T, D = x.shape
    depth = w_in.shape[0]
    assert (D, ctx.shape[1]) == (D_MODEL, CTX_LEN) and T % TOK_TILE == 0 and T >= BAND

    mod_rows = -(-(B + 1) // 8) * 8
    cc = jnp.concatenate([c, c_ctx[None, :], jnp.zeros((mod_rows - B - 1, D), c.dtype)], axis=0)
    mod = _modulation(cc, ada_w, ada_b)
    tab = _rope_tables(T)
    fnw = final_norm_w.reshape(1, D)
    two_heads = lambda w: jnp.tile(w, 2).reshape(1, LANES)

    for l in range(depth):
        update_ctx = l < depth - 1
        wkv, wq, wuq, wukv, wos, woa = _layer_weights(w_in[l], mla_w_uq[l], mla_w_ukv[l], w_o_swa[l], w_o_ax[l])
        proj_w = (wkv, wq, wuq, wukv, mla_q_norm[l].reshape(1, -1), mla_kv_norm[l].reshape(1, -1),
                  two_heads(ax_q_norm[l]), two_heads(ax_k_norm[l]))
        merge_w = (w_o_mla[l].astype(_BF16), wos, woa, w_out[l].astype(_BF16))
        nw = norm_w[l].reshape(1, D)
        modx = mod[l, :B].reshape(B, 1, 3 * D)
        modc = mod[l, B:B + 1].reshape(1, 1, 3 * D)

        pc = _project(ctx, modc, nw, None, proj_w, tile=CTX_LEN, with_q=update_ctx, name="project_ctx")
        px = _project(x, modx, nw, tab, proj_w, tile=TOK_TILE, with_q=True, name="project")
        kc3, vc3 = pc[0:6:2], pc[1:6:2]
        k3, v3 = px[0:6:2], px[1:6:2]
        u3 = _attend_all(swa_sink[l], px[6:9], px[9], kc3, vc3, k3, v3, tag="")
        if update_ctx:
            uc3 = _attend_all(swa_sink[l], pc[6:9], pc[9], kc3, vc3, tag="_ctx")
            ctx = _merge(uc3, pc[10], merge_w, ctx, modc, tile=CTX_LEN, name="merge_ctx")
            x = _merge(u3, px[10], merge_w, x, modx, tile=TOK_TILE, name="merge")
        else:
            x = _merge(u3, px[10], merge_w, x, modx, fnw, tile=TOK_TILE, name="merge_final")
    return x
```

```python
import functools

import jax
import jax.numpy as jnp
from jax import lax
from jax.experimental import pallas as pl
from jax.experimental.pallas import tpu as pltpu

D_MODEL = 1024
CTX_LEN = 256
GRID_W = 64
ROPE_THETA = 10000.0
RMS_EPS = 1e-6
NEG_INF = -1e30
N_BRANCH = 3
WINDOW = 128

HEADS = 8
KV_HEADS = 2
HEAD_DIM = 64
MLA_ROPE = 32
MLA_Q_RANK = 384
MLA_KV_RANK = 256
LOG2E = 1.4426950408889634
MLA_SCALE = (HEAD_DIM + MLA_ROPE) ** -0.5 * LOG2E
GQA_SCALE = HEAD_DIM ** -0.5 * LOG2E
BRANCH_W = HEADS * HEAD_DIM

LANES = 128
N_PAIRS = BRANCH_W // LANES
MLA_QK_W = 2 * LANES

TOK_TILE = 512
MERGE_TILE = 1024
Q_TILE = 256
BAND = Q_TILE + 2 * WINDOW
KEY_TILE = 256

KV_W = MLA_KV_RANK + 5 * LANES
Z_W = 3 * BRANCH_W
G_W = N_BRANCH * D_MODEL
TAB_W = 6 * LANES

VMEM_LIMIT = 56 * 1024 * 1024

_F32 = jnp.float32
_BF16 = jnp.bfloat16


def _params(n_axes):
    return pltpu.CompilerParams(
        dimension_semantics=("arbitrary",) * n_axes, vmem_limit_bytes=VMEM_LIMIT)


def _const_spec(shape):
    nd = len(shape)
    return pl.BlockSpec(shape, lambda *_: (0,) * nd, pipeline_mode=pl.Buffered(1))


def _mod_kernel(c_ref, w_ref, b_ref, o_ref):
    c = c_ref[...]
    s = c / (1.0 + jnp.exp(-c))
    o_ref[0] = jnp.dot(s, w_ref[0], preferred_element_type=_F32,
                       precision=lax.Precision.HIGHEST) + b_ref[0]


def _modulation(cc, ada_w, ada_b):
    depth = ada_w.shape[0]
    rows = cc.shape[0]
    col_tile = D_MODEL
    return pl.pallas_call(
        _mod_kernel,
        grid=(depth, (3 * D_MODEL) // col_tile),
        in_specs=[
            pl.BlockSpec((rows, D_MODEL), lambda l, n: (0, 0)),
            pl.BlockSpec((1, D_MODEL, col_tile), lambda l, n: (l, 0, n)),
            pl.BlockSpec((1, 1, col_tile), lambda l, n: (l, 0, n)),
        ],
        out_specs=pl.BlockSpec((1, rows, col_tile), lambda l, n: (l, 0, n)),
        out_shape=jax.ShapeDtypeStruct((depth, rows, 3 * D_MODEL), _F32),
        compiler_params=_params(2),
        name="modulation",
    )(cc, ada_w, ada_b.reshape(depth, 1, 3 * D_MODEL))


def _rms(x, w):
    ms = jnp.mean(x * x, axis=-1, keepdims=True)
    return x * lax.rsqrt(ms + RMS_EPS) * w


def _head_rms(x, w):
    lane = lax.broadcasted_iota(jnp.int32, x.shape, 1)
    y = x * x
    s = 1
    while s < HEAD_DIM:
        up = pltpu.roll(y, s, 1)
        dn = pltpu.roll(y, LANES - s, 1)
        y = y + jnp.where((lane & s) != 0, up, dn)
        s *= 2
    return x * lax.rsqrt(y * (1.0 / HEAD_DIM) + RMS_EPS) * w


def _rope(x, tab_ref, wide):
    if tab_ref is None:
        return x
    k, r = (0, HEAD_DIM // 4) if wide else (1, MLA_ROPE // 4)
    cos = tab_ref[:, (3 * k) * LANES:(3 * k + 1) * LANES]
    sinm = tab_ref[:, (3 * k + 1) * LANES:(3 * k + 2) * LANES]
    sinp = tab_ref[:, (3 * k + 2) * LANES:(3 * k + 3) * LANES]
    return x * cos + pltpu.roll(x, LANES - r, 1) * sinm + pltpu.roll(x, r, 1) * sinp


def _proj_kernel(*refs, has_rope, with_q):
    refs = list(refs)
    x_ref, mod_ref, nw_ref = refs[:3]
    del refs[:3]
    tab_ref = refs.pop(0) if has_rope else None
    wkv_ref, wq_ref, wuq_ref, wukv_ref, qn_ref, kvn_ref, axq_ref, axk_ref = refs[:8]
    km_ref, vm_ref, kv4_ref = refs[8:11]

    mod = mod_ref[0]
    x = x_ref[0]
    gain = nw_ref[...] * (1.0 + mod[:, D_MODEL:2 * D_MODEL])
    inv = lax.rsqrt(jnp.mean(x * x, axis=-1, keepdims=True) + RMS_EPS)
    h = (x * inv * gain + mod[:, :D_MODEL]).astype(_BF16)

    pkv = jnp.dot(h, wkv_ref[...], preferred_element_type=_F32)
    ckvn = _rms(pkv[:, :MLA_KV_RANK], kvn_ref[...]).astype(_BF16)
    kvm = jnp.dot(ckvn, wukv_ref[...], preferred_element_type=_F32)
    vm_ref[0] = kvm[:, BRANCH_W:].astype(_BF16)
    o = MLA_KV_RANK
    kr2 = _rope(pkv[:, o:o + LANES], tab_ref, False).astype(_BF16)
    for p in range(N_PAIRS):
        kn = kvm[:, p * LANES:(p + 1) * LANES]
        lo, hi = p * MLA_QK_W, p * MLA_QK_W + LANES
        if has_rope:
            km_ref[0, lo:hi, :] = kn.T.astype(_BF16)
            km_ref[0, hi:hi + LANES, :] = kr2.T
        else:
            km_ref[0, :, lo:hi] = kn.astype(_BF16)
            km_ref[0, :, hi:hi + LANES] = kr2
    o += LANES
    kv4_ref[0, :, :LANES] = _rope(pkv[:, o:o + LANES], tab_ref, True).astype(_BF16)
    o += LANES
    kv4_ref[0, :, LANES:2 * LANES] = pkv[:, o:o + LANES].astype(_BF16)
    o += LANES
    kv4_ref[0, :, 2 * LANES:3 * LANES] = _rope(
        _head_rms(pkv[:, o:o + LANES], axk_ref[...]), tab_ref, True).astype(_BF16)
    o += LANES
    kv4_ref[0, :, 3 * LANES:] = pkv[:, o:o + LANES].astype(_BF16)
    if not with_q:
        return

    qm_ref, qs_ref, qa_ref, z_ref, sg_ref = refs[11:]
    cq = jnp.dot(h, wq_ref[:, :MLA_Q_RANK], preferred_element_type=_F32)
    cqn = _rms(cq, qn_ref[...]).astype(_BF16)
    qm = jnp.dot(cqn, wuq_ref[...], preferred_element_type=_F32)
    for p in range(N_PAIRS):
        lo = p * MLA_QK_W
        qm_ref[0, :, lo:lo + LANES] = (qm[:, lo:lo + LANES] * MLA_SCALE).astype(_BF16)
        qm_ref[0, :, lo + LANES:lo + MLA_QK_W] = (
            _rope(qm[:, lo + LANES:lo + MLA_QK_W], tab_ref, False) * MLA_SCALE).astype(_BF16)
    c0 = MLA_Q_RANK
    sq = jnp.dot(h, wq_ref[:, c0:c0 + BRANCH_W], preferred_element_type=_F32)
    c0 += BRANCH_W
    aq = jnp.dot(h, wq_ref[:, c0:c0 + BRANCH_W], preferred_element_type=_F32)
    c0 += BRANCH_W
    for p in range(N_PAIRS):
        blk = slice(p * LANES, (p + 1) * LANES)
        qs_ref[0, :, blk] = (_rope(sq[:, blk], tab_ref, True) * GQA_SCALE).astype(_BF16)
        aqn = _head_rms(aq[:, blk], axq_ref[...])
        qa_ref[0, :, blk] = (_rope(aqn, tab_ref, True) * GQA_SCALE).astype(_BF16)
    for n in range(Z_W // BRANCH_W):
        z = jnp.dot(h, wq_ref[:, c0:c0 + BRANCH_W], preferred_element_type=_F32)
        z_ref[0, :, n * BRANCH_W:(n + 1) * BRANCH_W] = (z / (1.0 + jnp.exp(-z))).astype(_BF16)
        c0 += BRANCH_W
    for n in range(G_W // BRANCH_W):
        g = jnp.dot(h, wq_ref[:, c0:c0 + BRANCH_W], preferred_element_type=_F32)
        sg_ref[0, :, n * BRANCH_W:(n + 1) * BRANCH_W] = (1.0 / (1.0 + jnp.exp(-g))).astype(_BF16)
        c0 += BRANCH_W


def _project(x, mod, norm_w, tab, weights, *, tile, with_q, name):
    B, N, D = x.shape
    mod_map = (lambda j, b: (b, 0, 0)) if mod.shape[0] == B else (lambda j, b: (0, 0, 0))
    widths = [N_PAIRS * MLA_QK_W, BRANCH_W, 4 * LANES]
    if with_q:
        widths += [N_PAIRS * MLA_QK_W, BRANCH_W, BRANCH_W, Z_W, G_W]
    in_specs = [pl.BlockSpec((1, tile, D), lambda j, b: (b, j, 0)),
                pl.BlockSpec((1, 1, 3 * D), mod_map),
                _const_spec(norm_w.shape)]
    args = [x, mod, norm_w]
    if tab is not None:
        in_specs.append(pl.BlockSpec((tile, TAB_W), lambda j, b: (j, 0)))
        args.append(tab)
    in_specs += [_const_spec(w.shape) for w in weights]
    out_specs = [pl.BlockSpec((1, tile, w), lambda j, b: (b, j, 0)) for w in widths]
    out_shape = [jax.ShapeDtypeStruct((B, N, w), _BF16) for w in widths]
    if tab is not None:
        out_specs[0] = pl.BlockSpec((1, widths[0], tile), lambda j, b: (b, 0, j))
        out_shape[0] = jax.ShapeDtypeStruct((B, widths[0], N), _BF16)
    return pl.pallas_call(
        functools.partial(_proj_kernel, has_rope=tab is not None, with_q=with_q),
        grid=(N // tile, B),
        in_specs=in_specs,
        out_specs=out_specs,
        out_shape=out_shape,
        compiler_params=_params(2),
        name=name,
    )(*args, *weights)


def _scores(q, k):
    return lax.dot_general(q, k, (((1,), (1,)), ((), ())), preferred_element_type=_F32)


def _stack_heads(q, mla):
    lane = lax.broadcasted_iota(jnp.int32, q.shape, 1)
    is_a = lane < HEAD_DIM
    if mla:
        is_a = is_a | ((lane >= LANES) & (lane < LANES + MLA_ROPE))
    zero = jnp.zeros_like(q)
    return jnp.concatenate([jnp.where(is_a, q, zero), jnp.where(is_a, zero, q)], axis=0)


def _fill_v1(v1_scr, v_ref, n_blocks):
    n = v_ref.shape[1]
    for j in range(n_blocks):
        v1_scr[j, :, :LANES] = v_ref[0, :, j * LANES:(j + 1) * LANES]
        v1_scr[j, :, LANES:] = jnp.ones((n, LANES), _BF16)


def _online_softmax_pv(q2, key_tiles, sink=None):
    m = acc = None
    if sink is not None:
        lane = lax.broadcasted_iota(jnp.int32, (q2.shape[0], 2 * LANES), 1)
        m, acc = sink, jnp.where(lane < LANES, 0.0, 1.0)
    for k, v1, bias, k_is_transposed in key_tiles:
        s = jnp.dot(q2, k, preferred_element_type=_F32) if k_is_transposed else _scores(q2, k)
        if bias is not None:
            s = s + bias
        m_new = jnp.max(s, axis=-1, keepdims=True)
        if m is not None:
            m_new = jnp.maximum(m, m_new)
        pv = jnp.dot(jnp.exp2(s - m_new).astype(_BF16), v1, preferred_element_type=_F32)
        acc = pv if acc is None else jnp.exp2(m - m_new) * acc + pv
        m = m_new
    return acc[:, :LANES] / acc[:, LANES:]


def _gate_store(o_ref, z_ref, rows, p, r):
    n = r.shape[0] // 2
    lane = lax.broadcasted_iota(jnp.int32, (n, LANES), 1)
    o = jnp.where(lane < HEAD_DIM, r[:n], r[n:])
    blk = slice(p * LANES, (p + 1) * LANES)
    o_ref[0, rows, blk] = (o * z_ref[0, rows, blk].astype(_F32)).astype(_BF16)


def _tile_rows(tile):
    return pl.ds(pl.multiple_of(tile * Q_TILE, Q_TILE), Q_TILE)


def _attn_kernel(*refs, mla, window, latent_keys):
    refs = list(refs)
    sink_ref = refs.pop(0) if window else None
    q_ref, z_ref, kc_ref, vc_ref = refs[:4]
    k_ref, v_ref = refs[4:6] if latent_keys else (None, None)
    o_ref, vc1_scr = refs[-2 - latent_keys], refs[-1 - latent_keys]
    v1_scr = refs[-1] if latent_keys else None
    wq = MLA_QK_W if mla else LANES
    n_blocks = N_PAIRS if mla else 1
    n_tiles = q_ref.shape[1] // Q_TILE
    _fill_v1(vc1_scr, vc_ref, n_blocks)
    if latent_keys:
        _fill_v1(v1_scr, v_ref, n_blocks)
        n_lat = v_ref.shape[1]

    def ctx_tile(p):
        kb = slice(p * wq, (p + 1) * wq) if mla else slice(None)
        return (kc_ref[0, :, kb], vc1_scr[p if mla else 0], None, False)

    def latent_tile(p, start, bias):
        keys = pl.ds(start, KEY_TILE)
        if mla:
            return (k_ref[0, p * wq:(p + 1) * wq, keys], v1_scr[p, keys, :], bias, True)
        return (k_ref[0, keys, :], v1_scr[0, keys, :], bias, False)

    def pair_sink(p):
        row = lax.broadcasted_iota(jnp.int32, (2 * Q_TILE, 1), 0)
        return jnp.where(row < Q_TILE, sink_ref[p], sink_ref[N_PAIRS + p]) * LOG2E

    def step(t, carry):
        if window and latent_keys:
            q0 = t * Q_TILE
            k0 = jnp.clip(q0 - WINDOW, 0, n_lat - BAND)
            shape = (2 * Q_TILE, KEY_TILE)
            col = lax.broadcasted_iota(jnp.int32, shape, 1)
            row = lax.broadcasted_iota(jnp.int32, shape, 0) & (Q_TILE - 1)
            lat = [(pl.multiple_of(k0 + b0, LANES),
                    jnp.where(jnp.abs((col + (k0 + b0)) - (row + q0)) <= WINDOW, 0.0, NEG_INF))
                   for b0 in range(0, BAND, KEY_TILE)]
        elif latent_keys:
            lat = [(k0, None) for k0 in range(0, n_lat, KEY_TILE)]
        else:
            lat = []
        rows = _tile_rows(t)
        for p in range(N_PAIRS):
            tiles = [latent_tile(p, start, bias) for start, bias in lat] + [ctx_tile(p)]
            q2 = _stack_heads(q_ref[0, rows, p * wq:(p + 1) * wq], mla)
            r = _online_softmax_pv(q2, tiles, pair_sink(p) if window else None)
            _gate_store(o_ref, z_ref, rows, p, r)
        return carry

    lax.fori_loop(0, n_tiles, step, 0, unroll=min(n_tiles, 4 if window else 2))


def _attention(q, z, z_blk, kc, vc, k=None, v=None, *, mla=False, sink=None, name):
    B, nq, _ = q.shape
    latent_keys = k is not None

    def kv_spec(pair):
        a, blk = pair
        if mla:
            return pl.BlockSpec((1,) + a.shape[1:], lambda b: (b, 0, 0))
        return pl.BlockSpec((1, a.shape[1], LANES), lambda b: (b, 0, blk))

    args, in_specs = [], []
    if sink is not None:
        args.append(sink)
        in_specs.append(pl.BlockSpec(memory_space=pltpu.SMEM))
    args += [q, z, kc[0], vc[0]]
    in_specs += [pl.BlockSpec((1,) + q.shape[1:], lambda b: (b, 0, 0)),
                 pl.BlockSpec((1, nq, BRANCH_W), lambda b: (b, 0, z_blk)), kv_spec(kc), kv_spec(vc)]
    n_blocks = N_PAIRS if mla else 1
    scratch = [pltpu.VMEM((n_blocks, CTX_LEN, 2 * LANES), _BF16)]
    if latent_keys:
        args += [k[0], v[0]]
        in_specs += [kv_spec(k), kv_spec(v)]
        scratch.append(pltpu.VMEM((n_blocks, v[0].shape[1], 2 * LANES), _BF16))
    return pl.pallas_call(
        functools.partial(_attn_kernel, mla=mla, window=sink is not None, latent_keys=latent_keys),
        grid=(B,),
        in_specs=in_specs,
        out_specs=pl.BlockSpec((1, nq, BRANCH_W), lambda b: (b, 0, 0)),
        out_shape=jax.ShapeDtypeStruct((B, nq, BRANCH_W), _BF16),
        scratch_shapes=scratch,
        compiler_params=_params(1),
        name=name,
    )(*args)


def _merge_kernel(um_ref, us_ref, ua_ref, sg_ref, wom_ref, wos_ref, woa_ref, wout_ref,
                  x_ref, mod_ref, *rest, final_norm):
    y = None
    for n, (u_ref, w_ref) in enumerate(((um_ref, wom_ref), (us_ref, wos_ref), (ua_ref, woa_ref))):
        yb = jnp.dot(u_ref[0], w_ref[...], preferred_element_type=_F32)
        yb = yb * sg_ref[0, :, n * D_MODEL:(n + 1) * D_MODEL].astype(_F32)
        y = yb if y is None else y + yb
    out = jnp.dot(y.astype(_BF16), wout_ref[...], preferred_element_type=_F32)
    xn = x_ref[0] + mod_ref[0][:, 2 * D_MODEL:] * out
    if final_norm:
        fnw_ref, o_ref = rest
        o_ref[0] = _rms(xn, fnw_ref[...])
    else:
        (o_ref,) = rest
        o_ref[0] = xn


def _merge(us, sg, weights, x, mod, fnw=None, *, tile, name):
    B, N, D = x.shape
    mod_map = (lambda b, j: (b, 0, 0)) if mod.shape[0] == B else (lambda b, j: (0, 0, 0))
    tok = lambda w: pl.BlockSpec((1, tile, w), lambda b, j: (b, j, 0))
    in_specs = ([tok(BRANCH_W)] * 3 + [tok(G_W)] + [_const_spec(w.shape) for w in weights]
                + [tok(D), pl.BlockSpec((1, 1, 3 * D), mod_map)])
    args = [*us, sg, *weights, x, mod]
    if fnw is not None:
        in_specs.append(_const_spec(fnw.shape))
        args.append(fnw)
    return pl.pallas_call(
        functools.partial(_merge_kernel, final_norm=fnw is not None),
        grid=(B, N // tile),
        in_specs=in_specs,
        out_specs=tok(D),
        out_shape=jax.ShapeDtypeStruct((B, N, D), _F32),
        compiler_params=_params(2),
        name=name,
    )(*args)


def _axial_tables(d, n_lat):
    h = d // 2
    t = jnp.arange(n_lat, dtype=jnp.int32)
    pos_row = (t // GRID_W).astype(_F32)
    pos_col = (t % GRID_W).astype(_F32)
    freqs = ROPE_THETA ** (-jnp.arange(0, h, 2, dtype=_F32) / h)
    ang_r = pos_row[:, None] * freqs[None, :]
    ang_c = pos_col[:, None] * freqs[None, :]
    cr, sr, cc, sc = jnp.cos(ang_r), jnp.sin(ang_r), jnp.cos(ang_c), jnp.sin(ang_c)
    z = jnp.zeros_like(sr)
    cos = jnp.concatenate([cr, cr, cc, cc], axis=-1)
    sinm = jnp.concatenate([-sr, z, -sc, z], axis=-1)
    sinp = jnp.concatenate([z, sr, z, sc], axis=-1)
    return cos, sinm, sinp


def _rope_tables(n_lat):
    set64 = [jnp.tile(t, (1, 2)) for t in _axial_tables(HEAD_DIM, n_lat)]
    pad = LANES - 2 * MLA_ROPE
    set32 = [jnp.concatenate([t, t, jnp.full((n_lat, pad), f, _F32)], axis=1)
             for t, f in zip(_axial_tables(MLA_ROPE, n_lat), (1.0, 0.0, 0.0))]
    return jnp.concatenate(set64 + set32, axis=1)


def _permute_heads(w, axis):
    shape = w.shape
    w = w.reshape(shape[:axis] + (KV_HEADS, HEADS // KV_HEADS, HEAD_DIM) + shape[axis + 1:])
    return jnp.swapaxes(w, axis, axis + 1).reshape(shape)


def _layer_weights(w_in, w_uq, w_ukv, w_o_swa, w_o_ax):
    D = w_in.shape[0]
    offs = {}
    o = 0
    for name, w in (("ckv", MLA_KV_RANK), ("kr", MLA_ROPE), ("sk", LANES), ("sv", LANES),
                    ("ak", LANES), ("av", LANES), ("cq", MLA_Q_RANK), ("sq", BRANCH_W),
                    ("aq", BRANCH_W), ("zm", BRANCH_W), ("zs", BRANCH_W), ("za", BRANCH_W),
                    ("g", G_W)):
        offs[name] = w_in[:, o:o + w]
        o += w
    kr = offs["kr"]
    kr2 = jnp.concatenate([kr, kr, jnp.zeros((D, LANES - 2 * MLA_ROPE), w_in.dtype)], axis=1)
    wkv = jnp.concatenate([offs["ckv"], kr2, offs["sk"], offs["sv"], offs["ak"], offs["av"]], axis=1)
    wq = jnp.concatenate([offs["cq"], _permute_heads(offs["sq"], 1), _permute_heads(offs["aq"], 1),
                          offs["zm"], _permute_heads(offs["zs"], 1), _permute_heads(offs["za"], 1),
                          offs["g"]], axis=1)
    uq = w_uq.reshape(MLA_Q_RANK, N_PAIRS, 2, HEAD_DIM + MLA_ROPE)
    nope = uq[..., :HEAD_DIM].reshape(MLA_Q_RANK, N_PAIRS, 2 * HEAD_DIM)
    rope = uq[..., HEAD_DIM:].reshape(MLA_Q_RANK, N_PAIRS, 2 * MLA_ROPE)
    pad = jnp.zeros((MLA_Q_RANK, N_PAIRS, MLA_QK_W - 2 * HEAD_DIM - 2 * MLA_ROPE), w_uq.dtype)
    wuq = jnp.concatenate([nope, rope, pad], axis=-1).reshape(MLA_Q_RANK, N_PAIRS * MLA_QK_W)
    ukv = w_ukv.reshape(MLA_KV_RANK, HEADS, 2, HEAD_DIM)
    wukv = jnp.concatenate([ukv[:, :, 0].reshape(MLA_KV_RANK, BRANCH_W),
                            ukv[:, :, 1].reshape(MLA_KV_RANK, BRANCH_W)], axis=1)
    bf = lambda a: a.astype(_BF16)
    return (bf(wkv), bf(wq), bf(wuq), bf(wukv),
            bf(_permute_heads(w_o_swa, 0)), bf(_permute_heads(w_o_ax, 0)))


def _attend_all(sink, pq, pc, px=None, *, tag):
    qm, qs, qa, z = pq[3:7]
    lat = lambda i, blk: None if px is None else (px[i], blk)
    return (_attention(qm, z, 0, (pc[0], 0), (pc[1], 0), lat(0, 0), lat(1, 0), mla=True, name="attn_mla" + tag),
            _attention(qs, z, 1, (pc[2], 0), (pc[2], 1), lat(2, 0), lat(2, 1), sink=sink, name="attn_window" + tag),
            _attention(qa, z, 2, (pc[2], 2), (pc[2], 3), lat(2, 2), lat(2, 3), name="attn_axial" + tag))


def kernel(x, c, ctx, c_ctx, ada_w, ada_b, norm_w, w_in, mla_q_norm, mla_w_uq, mla_kv_norm, mla_w_ukv,
           swa_sink, ax_q_norm, ax_k_norm, w_o_mla, w_o_swa, w_o_ax, w_out, final_norm_w):
    B, T, D = x.shape
    depth = w_in.shape[0]
    merge_tile = min(MERGE_TILE, T)
    assert (D, ctx.shape[1]) == (D_MODEL, CTX_LEN) and T % TOK_TILE == 0 and T % merge_tile == 0 and T >= BAND

    mod_rows = -(-(B + 1) // 8) * 8
    cc = jnp.concatenate([c, c_ctx[None, :], jnp.zeros((mod_rows - B - 1, D), c.dtype)], axis=0)
    mod = _modulation(cc, ada_w, ada_b)
    tab = _rope_tables(T)
    fnw = final_norm_w.reshape(1, D)
    two_heads = lambda w: jnp.tile(w, 2).reshape(1, LANES)

    for l in range(depth):
        update_ctx = l < depth - 1
        wkv, wq, wuq, wukv, wos, woa = _layer_weights(w_in[l], mla_w_uq[l], mla_w_ukv[l], w_o_swa[l], w_o_ax[l])
        proj_w = (wkv, wq, wuq, wukv, mla_q_norm[l].reshape(1, -1), mla_kv_norm[l].reshape(1, -1),
                  two_heads(ax_q_norm[l]), two_heads(ax_k_norm[l]))
        merge_w = (w_o_mla[l].astype(_BF16), wos, woa, w_out[l].astype(_BF16))
        nw = norm_w[l].reshape(1, D)
        modx = mod[l, :B].reshape(B, 1, 3 * D)
        modc = mod[l, B:B + 1].reshape(1, 1, 3 * D)

        pc = _project(ctx, modc, nw, None, proj_w, tile=CTX_LEN, with_q=update_ctx, name="project_ctx")
        px = _project(x, modx, nw, tab, proj_w, tile=TOK_TILE, with_q=True, name="project")
        u3 = _attend_all(swa_sink[l], px, pc, px, tag="")
        if update_ctx:
            uc3 = _attend_all(swa_sink[l], pc, pc, tag="_ctx")
            ctx = _merge(uc3, pc[7], merge_w, ctx, modc, tile=CTX_LEN, name="merge_ctx")
            x = _merge(u3, px[7], merge_w, x, modx, tile=merge_tile, name="merge")
        else:
            x = _merge(u3, px[7], merge_w, x, modx, fnw, tile=merge_tile, name="merge_final")
    return x
```

```python
import functools

import jax
import jax.numpy as jnp
from jax import lax
from jax.experimental import pallas as pl
from jax.experimental.pallas import tpu as pltpu

D_MODEL = 1024
CTX_LEN = 256
GRID_W = 64
ROPE_THETA = 10000.0
RMS_EPS = 1e-6
NEG_INF = -1e30
N_BRANCH = 3
WINDOW = 128

HEADS = 8
KV_HEADS = 2
HEAD_DIM = 64
MLA_ROPE = 32
MLA_Q_RANK = 384
MLA_KV_RANK = 256
LOG2E = 1.4426950408889634
MLA_SCALE = (HEAD_DIM + MLA_ROPE) ** -0.5 * LOG2E
GQA_SCALE = HEAD_DIM ** -0.5 * LOG2E
BRANCH_W = HEADS * HEAD_DIM

LANES = 128
N_PAIRS = BRANCH_W // LANES
MLA_QK_W = 2 * LANES

TOK_TILE = 512
MERGE_TILE = 1024
CTX_ROWS_PER_STEP = 4
CTX_PROJ_ROWS_PER_STEP = 2
Q_TILE = 256
BAND = Q_TILE + 2 * WINDOW
KEY_TILE = 256

KV_W = MLA_KV_RANK + 5 * LANES
Z_W = 3 * BRANCH_W
G_W = N_BRANCH * D_MODEL
TAB_W = 6 * LANES

VMEM_LIMIT = 56 * 1024 * 1024

_F32 = jnp.float32
_BF16 = jnp.bfloat16


def _params(n_axes):
    return pltpu.CompilerParams(
        dimension_semantics=("arbitrary",) * n_axes, vmem_limit_bytes=VMEM_LIMIT)


def _const_spec(shape):
    nd = len(shape)
    return pl.BlockSpec(shape, lambda *_: (0,) * nd, pipeline_mode=pl.Buffered(1))


def _mod_kernel(c_ref, w_ref, b_ref, o_ref):
    c = c_ref[...]
    s = c / (1.0 + jnp.exp(-c))
    o_ref[0] = jnp.dot(s, w_ref[0], preferred_element_type=_F32,
                       precision=lax.Precision.HIGHEST) + b_ref[0]


def _modulation(cc, ada_w, ada_b):
    depth = ada_w.shape[0]
    rows = cc.shape[0]
    col_tile = D_MODEL
    return pl.pallas_call(
        _mod_kernel,
        grid=(depth, (3 * D_MODEL) // col_tile),
        in_specs=[
            pl.BlockSpec((rows, D_MODEL), lambda l, n: (0, 0)),
            pl.BlockSpec((1, D_MODEL, col_tile), lambda l, n: (l, 0, n)),
            pl.BlockSpec((1, 1, col_tile), lambda l, n: (l, 0, n)),
        ],
        out_specs=pl.BlockSpec((1, rows, col_tile), lambda l, n: (l, 0, n)),
        out_shape=jax.ShapeDtypeStruct((depth, rows, 3 * D_MODEL), _F32),
        compiler_params=_params(2),
        name="modulation",
    )(cc, ada_w, ada_b.reshape(depth, 1, 3 * D_MODEL))


def _rms(x, w):
    ms = jnp.mean(x * x, axis=-1, keepdims=True)
    return x * lax.rsqrt(ms + RMS_EPS) * w


def _head_rms(x, w):
    lane = lax.broadcasted_iota(jnp.int32, x.shape, 1)
    y = x * x
    s = 1
    while s < HEAD_DIM:
        up = pltpu.roll(y, s, 1)
        dn = pltpu.roll(y, LANES - s, 1)
        y = y + jnp.where((lane & s) != 0, up, dn)
        s *= 2
    return x * lax.rsqrt(y * (1.0 / HEAD_DIM) + RMS_EPS) * w


def _rope(x, tab_ref, wide):
    if tab_ref is None:
        return x
    k, r = (0, HEAD_DIM // 4) if wide else (1, MLA_ROPE // 4)
    cos = tab_ref[:, (3 * k) * LANES:(3 * k + 1) * LANES]
    sinm = tab_ref[:, (3 * k + 1) * LANES:(3 * k + 2) * LANES]
    sinp = tab_ref[:, (3 * k + 2) * LANES:(3 * k + 3) * LANES]
    return x * cos + pltpu.roll(x, LANES - r, 1) * sinm + pltpu.roll(x, r, 1) * sinp


def _proj_kernel(*refs, has_rope, with_q):
    refs = list(refs)
    x_ref, mod_ref, nw_ref = refs[:3]
    del refs[:3]
    tab_ref = refs.pop(0) if has_rope else None
    wkv_ref, wq_ref, wuq_ref, wukv_ref, qn_ref, kvn_ref, axq_ref, axk_ref = refs[:8]
    km_ref, vm_ref, kv4_ref = refs[8:11]

    nb, tile, _ = x_ref.shape

    def put(ref, cols, val):
        ref[:, :, cols] = val.reshape(nb, tile, val.shape[-1])

    mod = mod_ref[0]
    x = x_ref[...].reshape(nb * tile, D_MODEL)
    gain = nw_ref[...] * (1.0 + mod[:, D_MODEL:2 * D_MODEL])
    inv = lax.rsqrt(jnp.mean(x * x, axis=-1, keepdims=True) + RMS_EPS)
    h = (x * inv * gain + mod[:, :D_MODEL]).astype(_BF16)

    pkv = jnp.dot(h, wkv_ref[...], preferred_element_type=_F32)
    ckvn = _rms(pkv[:, :MLA_KV_RANK], kvn_ref[...]).astype(_BF16)
    kvm = jnp.dot(ckvn, wukv_ref[...], preferred_element_type=_F32)
    put(vm_ref, slice(None), kvm[:, BRANCH_W:].astype(_BF16))
    o = MLA_KV_RANK
    kr2 = _rope(pkv[:, o:o + LANES], tab_ref, False).astype(_BF16)
    for p in range(N_PAIRS):
        kn = kvm[:, p * LANES:(p + 1) * LANES]
        lo, hi = p * MLA_QK_W, p * MLA_QK_W + LANES
        if has_rope:
            km_ref[0, lo:hi, :] = kn.T.astype(_BF16)
            km_ref[0, hi:hi + LANES, :] = kr2.T
        else:
            put(km_ref, slice(lo, hi), kn.astype(_BF16))
            put(km_ref, slice(hi, hi + LANES), kr2)
    o += LANES
    put(kv4_ref, slice(0, LANES), _rope(pkv[:, o:o + LANES], tab_ref, True).astype(_BF16))
    o += LANES
    put(kv4_ref, slice(LANES, 2 * LANES), pkv[:, o:o + LANES].astype(_BF16))
    o += LANES
    put(kv4_ref, slice(2 * LANES, 3 * LANES),
        _rope(_head_rms(pkv[:, o:o + LANES], axk_ref[...]), tab_ref, True).astype(_BF16))
    o += LANES
    put(kv4_ref, slice(3 * LANES, 4 * LANES), pkv[:, o:o + LANES].astype(_BF16))
    if not with_q:
        return

    qm_ref, qs_ref, qa_ref, z_ref, sg_ref = refs[11:]
    cq = jnp.dot(h, wq_ref[:, :MLA_Q_RANK], preferred_element_type=_F32)
    cqn = _rms(cq, qn_ref[...]).astype(_BF16)
    qm = jnp.dot(cqn, wuq_ref[...], preferred_element_type=_F32)
    for p in range(N_PAIRS):
        lo = p * MLA_QK_W
        put(qm_ref, slice(lo, lo + LANES), (qm[:, lo:lo + LANES] * MLA_SCALE).astype(_BF16))
        put(qm_ref, slice(lo + LANES, lo + MLA_QK_W),
            (_rope(qm[:, lo + LANES:lo + MLA_QK_W], tab_ref, False) * MLA_SCALE).astype(_BF16))
    c0 = MLA_Q_RANK
    sq = jnp.dot(h, wq_ref[:, c0:c0 + BRANCH_W], preferred_element_type=_F32)
    c0 += BRANCH_W
    aq = jnp.dot(h, wq_ref[:, c0:c0 + BRANCH_W], preferred_element_type=_F32)
    c0 += BRANCH_W
    for p in range(N_PAIRS):
        blk = slice(p * LANES, (p + 1) * LANES)
        put(qs_ref, blk, (_rope(sq[:, blk], tab_ref, True) * GQA_SCALE).astype(_BF16))
        aqn = _head_rms(aq[:, blk], axq_ref[...])
        put(qa_ref, blk, (_rope(aqn, tab_ref, True) * GQA_SCALE).astype(_BF16))
    for n in range(Z_W // BRANCH_W):
        z = jnp.dot(h, wq_ref[:, c0:c0 + BRANCH_W], preferred_element_type=_F32)
        put(z_ref, slice(n * BRANCH_W, (n + 1) * BRANCH_W), (z / (1.0 + jnp.exp(-z))).astype(_BF16))
        c0 += BRANCH_W
    for n in range(G_W // BRANCH_W):
        g = jnp.dot(h, wq_ref[:, c0:c0 + BRANCH_W], preferred_element_type=_F32)
        put(sg_ref, slice(n * BRANCH_W, (n + 1) * BRANCH_W), (1.0 / (1.0 + jnp.exp(-g))).astype(_BF16))
        c0 += BRANCH_W


def _project(x, mod, norm_w, tab, weights, *, tile, with_q, name, nb=1):
    B, N, D = x.shape
    assert B % nb == 0 and (nb == 1 or (mod.shape[0] == 1 and tab is None))
    mod_map = (lambda j, b: (b, 0, 0)) if mod.shape[0] == B else (lambda j, b: (0, 0, 0))
    widths = [N_PAIRS * MLA_QK_W, BRANCH_W, 4 * LANES]
    if with_q:
        widths += [N_PAIRS * MLA_QK_W, BRANCH_W, BRANCH_W, Z_W, G_W]
    in_specs = [pl.BlockSpec((nb, tile, D), lambda j, b: (b, j, 0)),
                pl.BlockSpec((1, 1, 3 * D), mod_map),
                _const_spec(norm_w.shape)]
    args = [x, mod, norm_w]
    if tab is not None:
        in_specs.append(pl.BlockSpec((tile, TAB_W), lambda j, b: (j, 0)))
        args.append(tab)
    in_specs += [_const_spec(w.shape) for w in weights]
    out_specs = [pl.BlockSpec((nb, tile, w), lambda j, b: (b, j, 0)) for w in widths]
    out_shape = [jax.ShapeDtypeStruct((B, N, w), _BF16) for w in widths]
    if tab is not None:
        out_specs[0] = pl.BlockSpec((1, widths[0], tile), lambda j, b: (b, 0, j))
        out_shape[0] = jax.ShapeDtypeStruct((B, widths[0], N), _BF16)
    return pl.pallas_call(
        functools.partial(_proj_kernel, has_rope=tab is not None, with_q=with_q),
        grid=(N // tile, B // nb),
        in_specs=in_specs,
        out_specs=out_specs,
        out_shape=out_shape,
        compiler_params=_params(2),
        name=name,
    )(*args, *weights)


def _scores(q, k):
    return lax.dot_general(q, k, (((1,), (1,)), ((), ())), preferred_element_type=_F32)


def _stack_heads(q, mla):
    lane = lax.broadcasted_iota(jnp.int32, q.shape, 1)
    is_a = lane < HEAD_DIM
    if mla:
        is_a = is_a | ((lane >= LANES) & (lane < LANES + MLA_ROPE))
    zero = jnp.zeros_like(q)
    return jnp.concatenate([jnp.where(is_a, q, zero), jnp.where(is_a, zero, q)], axis=0)


def _fill_v1(v1_scr, v_ref, bi, n_blocks):
    n = v_ref.shape[1]
    for j in range(n_blocks):
        v1_scr[j, :, :LANES] = v_ref[bi, :, j * LANES:(j + 1) * LANES]
        v1_scr[j, :, LANES:] = jnp.ones((n, LANES), _BF16)


def _online_softmax_pv(q2, key_tiles, sink=None):
    m = acc = None
    for k, v1, bias, k_is_transposed in key_tiles:
        s = jnp.dot(q2, k, preferred_element_type=_F32) if k_is_transposed else _scores(q2, k)
        if bias is not None:
            s = s + bias
        m_new = jnp.max(s, axis=-1, keepdims=True)
        if m is not None:
            m_new = jnp.maximum(m, m_new)
        elif sink is not None:
            m_new = jnp.maximum(sink, m_new)
        pv = jnp.dot(jnp.exp2(s - m_new).astype(_BF16), v1, preferred_element_type=_F32)
        acc = pv if acc is None else jnp.exp2(m - m_new) * acc + pv
        m = m_new
    den = acc[:, LANES:]
    if sink is not None:
        den = den + jnp.exp2(sink - m)
    return acc[:, :LANES] / den


def _gate_store(o_ref, z_ref, bi, rows, p, r):
    n = r.shape[0] // 2
    lane = lax.broadcasted_iota(jnp.int32, (n, LANES), 1)
    o = jnp.where(lane < HEAD_DIM, r[:n], r[n:])
    blk = slice(p * LANES, (p + 1) * LANES)
    o_ref[bi, rows, blk] = (o * z_ref[bi, rows, blk].astype(_F32)).astype(_BF16)


def _tile_rows(tile):
    return pl.ds(pl.multiple_of(tile * Q_TILE, Q_TILE), Q_TILE)


def _attn_kernel(*refs, mla, window, latent_keys):
    refs = list(refs)
    sink_ref = refs.pop(0) if window else None
    q_ref, z_ref, kc_ref, vc_ref = refs[:4]
    k_ref, v_ref = refs[4:6] if latent_keys else (None, None)
    o_ref, vc1_scr = refs[-2 - latent_keys], refs[-1 - latent_keys]
    v1_scr = refs[-1] if latent_keys else None
    for bi in range(q_ref.shape[0]):
        _attn_row(bi, sink_ref, q_ref, z_ref, kc_ref, vc_ref, k_ref, v_ref, o_ref, vc1_scr, v1_scr,
                  mla=mla, window=window)


def _attn_row(bi, sink_ref, q_ref, z_ref, kc_ref, vc_ref, k_ref, v_ref, o_ref, vc1_scr, v1_scr, *, mla, window):
    latent_keys = k_ref is not None
    n_tiles = q_ref.shape[1] // Q_TILE
    wq = MLA_QK_W if mla else LANES
    n_blocks = N_PAIRS if mla else 1
    vc1_scr = vc1_scr.at[bi]
    _fill_v1(vc1_scr, vc_ref, bi, n_blocks)
    if latent_keys:
        _fill_v1(v1_scr, v_ref, bi, n_blocks)
        n_lat = v_ref.shape[1]

    def ctx_tile(p):
        kb = slice(p * wq, (p + 1) * wq) if mla else slice(None)
        return (kc_ref[bi, :, kb], vc1_scr[p if mla else 0], None, False)

    def latent_tile(p, start, bias):
        keys = pl.ds(start, KEY_TILE)
        if mla:
            return (k_ref[bi, p * wq:(p + 1) * wq, keys], v1_scr[p, keys, :], bias, True)
        return (k_ref[bi, keys, :], v1_scr[0, keys, :], bias, False)

    def pair_sink(p):
        row = lax.broadcasted_iota(jnp.int32, (2 * Q_TILE, 1), 0)
        return jnp.where(row < Q_TILE, sink_ref[p], sink_ref[N_PAIRS + p]) * LOG2E

    def step(t, carry):
        if window and latent_keys:
            q0 = t * Q_TILE
            k0 = jnp.clip(q0 - WINDOW, 0, n_lat - BAND)
            shape = (2 * Q_TILE, KEY_TILE)
            col = lax.broadcasted_iota(jnp.int32, shape, 1)
            row = lax.broadcasted_iota(jnp.int32, shape, 0) & (Q_TILE - 1)
            lat = [(pl.multiple_of(k0 + b0, LANES),
                    jnp.where(jnp.abs((col + (k0 + b0)) - (row + q0)) <= WINDOW, 0.0, NEG_INF))
                   for b0 in range(0, BAND, KEY_TILE)]
        elif latent_keys:
            lat = [(k0, None) for k0 in range(0, n_lat, KEY_TILE)]
        else:
            lat = []
        rows = _tile_rows(t)
        for p in range(N_PAIRS):
            tiles = [latent_tile(p, start, bias) for start, bias in lat] + [ctx_tile(p)]
            q2 = _stack_heads(q_ref[bi, rows, p * wq:(p + 1) * wq], mla)
            r = _online_softmax_pv(q2, tiles, pair_sink(p) if window else None)
            _gate_store(o_ref, z_ref, bi, rows, p, r)
        return carry

    lax.fori_loop(0, n_tiles, step, 0, unroll=min(n_tiles, 4 if window else 2))


def _attention(q, z, z_blk, kc, vc, k=None, v=None, *, mla=False, sink=None, name, nb=1):
    B, nq, _ = q.shape
    latent_keys = k is not None

    def kv_spec(pair):
        a, blk = pair
        if mla:
            return pl.BlockSpec((nb,) + a.shape[1:], lambda b: (b, 0, 0))
        return pl.BlockSpec((nb, a.shape[1], LANES), lambda b: (b, 0, blk))

    args, in_specs = [], []
    if sink is not None:
        args.append(sink)
        in_specs.append(pl.BlockSpec(memory_space=pltpu.SMEM))
    args += [q, z, kc[0], vc[0]]
    in_specs += [pl.BlockSpec((nb,) + q.shape[1:], lambda b: (b, 0, 0)),
                 pl.BlockSpec((nb, nq, BRANCH_W), lambda b: (b, 0, z_blk)), kv_spec(kc), kv_spec(vc)]
    n_blocks = N_PAIRS if mla else 1
    scratch = [pltpu.VMEM((nb, n_blocks, CTX_LEN, 2 * LANES), _BF16)]
    if latent_keys:
        args += [k[0], v[0]]
        in_specs += [kv_spec(k), kv_spec(v)]
        scratch.append(pltpu.VMEM((n_blocks, v[0].shape[1], 2 * LANES), _BF16))
    return pl.pallas_call(
        functools.partial(_attn_kernel, mla=mla, window=sink is not None, latent_keys=latent_keys),
        grid=(B // nb,),
        in_specs=in_specs,
        out_specs=pl.BlockSpec((nb, nq, BRANCH_W), lambda b: (b, 0, 0)),
        out_shape=jax.ShapeDtypeStruct((B, nq, BRANCH_W), _BF16),
        scratch_shapes=scratch,
        compiler_params=_params(1),
        name=name,
    )(*args)


def _merge_kernel(um_ref, us_ref, ua_ref, sg_ref, wom_ref, wos_ref, woa_ref, wout_ref,
                  x_ref, mod_ref, *rest, final_norm):
    nb, tile, _ = x_ref.shape
    rows = nb * tile
    y = None
    for n, (u_ref, w_ref) in enumerate(((um_ref, wom_ref), (us_ref, wos_ref), (ua_ref, woa_ref))):
        yb = jnp.dot(u_ref[...].reshape(rows, BRANCH_W), w_ref[...], preferred_element_type=_F32)
        yb = yb * sg_ref[:, :, n * D_MODEL:(n + 1) * D_MODEL].reshape(rows, D_MODEL).astype(_F32)
        y = yb if y is None else y + yb
    out = jnp.dot(y.astype(_BF16), wout_ref[...], preferred_element_type=_F32)
    xn = x_ref[...].reshape(rows, D_MODEL) + mod_ref[0][:, 2 * D_MODEL:] * out
    if final_norm:
        fnw_ref, o_ref = rest
        o_ref[...] = _rms(xn, fnw_ref[...]).reshape(nb, tile, D_MODEL)
    else:
        (o_ref,) = rest
        o_ref[...] = xn.reshape(nb, tile, D_MODEL)


def _merge(us, sg, weights, x, mod, fnw=None, *, tile, name, nb=1):
    B, N, D = x.shape
    assert B % nb == 0 and (nb == 1 or mod.shape[0] == 1)
    mod_map = (lambda b, j: (b, 0, 0)) if mod.shape[0] == B else (lambda b, j: (0, 0, 0))
    tok = lambda w: pl.BlockSpec((nb, tile, w), lambda b, j: (b, j, 0))
    in_specs = ([tok(BRANCH_W)] * 3 + [tok(G_W)] + [_const_spec(w.shape) for w in weights]
                + [tok(D), pl.BlockSpec((1, 1, 3 * D), mod_map)])
    args = [*us, sg, *weights, x, mod]
    if fnw is not None:
        in_specs.append(_const_spec(fnw.shape))
        args.append(fnw)
    return pl.pallas_call(
        functools.partial(_merge_kernel, final_norm=fnw is not None),
        grid=(B // nb, N // tile),
        in_specs=in_specs,
        out_specs=tok(D),
        out_shape=jax.ShapeDtypeStruct((B, N, D), _F32),
        compiler_params=_params(2),
        name=name,
    )(*args)


def _axial_tables(d, n_lat):
    h = d // 2
    t = jnp.arange(n_lat, dtype=jnp.int32)
    pos_row = (t // GRID_W).astype(_F32)
    pos_col = (t % GRID_W).astype(_F32)
    freqs = ROPE_THETA ** (-jnp.arange(0, h, 2, dtype=_F32) / h)
    ang_r = pos_row[:, None] * freqs[None, :]
    ang_c = pos_col[:, None] * freqs[None, :]
    cr, sr, cc, sc = jnp.cos(ang_r), jnp.sin(ang_r), jnp.cos(ang_c), jnp.sin(ang_c)
    z = jnp.zeros_like(sr)
    cos = jnp.concatenate([cr, cr, cc, cc], axis=-1)
    sinm = jnp.concatenate([-sr, z, -sc, z], axis=-1)
    sinp = jnp.concatenate([z, sr, z, sc], axis=-1)
    return cos, sinm, sinp


def _rope_tables(n_lat):
    set64 = [jnp.tile(t, (1, 2)) for t in _axial_tables(HEAD_DIM, n_lat)]
    pad = LANES - 2 * MLA_ROPE
    set32 = [jnp.concatenate([t, t, jnp.full((n_lat, pad), f, _F32)], axis=1)
             for t, f in zip(_axial_tables(MLA_ROPE, n_lat), (1.0, 0.0, 0.0))]
    return jnp.concatenate(set64 + set32, axis=1)


def _permute_heads(w, axis):
    shape = w.shape
    w = w.reshape(shape[:axis] + (KV_HEADS, HEADS // KV_HEADS, HEAD_DIM) + shape[axis + 1:])
    return jnp.swapaxes(w, axis, axis + 1).reshape(shape)


def _layer_weights(w_in, w_uq, w_ukv, w_o_swa, w_o_ax):
    D = w_in.shape[0]
    offs = {}
    o = 0
    for name, w in (("ckv", MLA_KV_RANK), ("kr", MLA_ROPE), ("sk", LANES), ("sv", LANES),
                    ("ak", LANES), ("av", LANES), ("cq", MLA_Q_RANK), ("sq", BRANCH_W),
                    ("aq", BRANCH_W), ("zm", BRANCH_W), ("zs", BRANCH_W), ("za", BRANCH_W),
                    ("g", G_W)):
        offs[name] = w_in[:, o:o + w]
        o += w
    kr = offs["kr"]
    kr2 = jnp.concatenate([kr, kr, jnp.zeros((D, LANES - 2 * MLA_ROPE), w_in.dtype)], axis=1)
    wkv = jnp.concatenate([offs["ckv"], kr2, offs["sk"], offs["sv"], offs["ak"], offs["av"]], axis=1)
    wq = jnp.concatenate([offs["cq"], _permute_heads(offs["sq"], 1), _permute_heads(offs["aq"], 1),
                          offs["zm"], _permute_heads(offs["zs"], 1), _permute_heads(offs["za"], 1),
                          offs["g"]], axis=1)
    uq = w_uq.reshape(MLA_Q_RANK, N_PAIRS, 2, HEAD_DIM + MLA_ROPE)
    nope = uq[..., :HEAD_DIM].reshape(MLA_Q_RANK, N_PAIRS, 2 * HEAD_DIM)
    rope = uq[..., HEAD_DIM:].reshape(MLA_Q_RANK, N_PAIRS, 2 * MLA_ROPE)
    pad = jnp.zeros((MLA_Q_RANK, N_PAIRS, MLA_QK_W - 2 * HEAD_DIM - 2 * MLA_ROPE), w_uq.dtype)
    wuq = jnp.concatenate([nope, rope, pad], axis=-1).reshape(MLA_Q_RANK, N_PAIRS * MLA_QK_W)
    ukv = w_ukv.reshape(MLA_KV_RANK, HEADS, 2, HEAD_DIM)
    wukv = jnp.concatenate([ukv[:, :, 0].reshape(MLA_KV_RANK, BRANCH_W),
                            ukv[:, :, 1].reshape(MLA_KV_RANK, BRANCH_W)], axis=1)
    bf = lambda a: a.astype(_BF16)
    return (bf(wkv), bf(wq), bf(wuq), bf(wukv),
            bf(_permute_heads(w_o_swa, 0)), bf(_permute_heads(w_o_ax, 0)))


def _attend_all(sink, pq, pc, px=None, *, tag, nb=1):
    qm, qs, qa, z = pq[3:7]
    lat = lambda i, blk: None if px is None else (px[i], blk)
    return (_attention(qm, z, 0, (pc[0], 0), (pc[1], 0), lat(0, 0), lat(1, 0), mla=True, name="attn_mla" + tag, nb=nb),
            _attention(qs, z, 1, (pc[2], 0), (pc[2], 1), lat(2, 0), lat(2, 1), sink=sink, name="attn_window" + tag, nb=nb),
            _attention(qa, z, 2, (pc[2], 2), (pc[2], 3), lat(2, 2), lat(2, 3), name="attn_axial" + tag, nb=nb))


def kernel(x, c, ctx, c_ctx, ada_w, ada_b, norm_w, w_in, mla_q_norm, mla_w_uq, mla_kv_norm, mla_w_ukv,
           swa_sink, ax_q_norm, ax_k_norm, w_o_mla, w_o_swa, w_o_ax, w_out, final_norm_w):
    B, T, D = x.shape
    depth = w_in.shape[0]
    merge_tile = min(MERGE_TILE, T)
    ctx_nb = CTX_ROWS_PER_STEP if B % CTX_ROWS_PER_STEP == 0 else 1
    ctx_proj_nb = CTX_PROJ_ROWS_PER_STEP if B % CTX_PROJ_ROWS_PER_STEP == 0 else 1
    assert (D, ctx.shape[1]) == (D_MODEL, CTX_LEN) and T % TOK_TILE == 0 and T % merge_tile == 0 and T >= BAND

    mod_rows = -(-(B + 1) // 8) * 8
    cc = jnp.concatenate([c, c_ctx[None, :], jnp.zeros((mod_rows - B - 1, D), c.dtype)], axis=0)
    mod = _modulation(cc, ada_w, ada_b)
    tab = _rope_tables(T)
    fnw = final_norm_w.reshape(1, D)
    two_heads = lambda w: jnp.tile(w, 2).reshape(1, LANES)

    for l in range(depth):
        update_ctx = l < depth - 1
        wkv, wq, wuq, wukv, wos, woa = _layer_weights(w_in[l], mla_w_uq[l], mla_w_ukv[l], w_o_swa[l], w_o_ax[l])
        proj_w = (wkv, wq, wuq, wukv, mla_q_norm[l].reshape(1, -1), mla_kv_norm[l].reshape(1, -1),
                  two_heads(ax_q_norm[l]), two_heads(ax_k_norm[l]))
        merge_w = (w_o_mla[l].astype(_BF16), wos, woa, w_out[l].astype(_BF16))
        nw = norm_w[l].reshape(1, D)
        modx = mod[l, :B].reshape(B, 1, 3 * D)
        modc = mod[l, B:B + 1].reshape(1, 1, 3 * D)

        pc = _project(ctx, modc, nw, None, proj_w, tile=CTX_LEN, with_q=update_ctx, name="project_ctx", nb=ctx_proj_nb)
        px = _project(x, modx, nw, tab, proj_w, tile=TOK_TILE, with_q=True, name="project")
        u3 = _attend_all(swa_sink[l], px, pc, px, tag="")
        if update_ctx:
            uc3 = _attend_all(swa_sink[l], pc, pc, tag="_ctx", nb=ctx_nb)
            ctx = _merge(uc3, pc[7], merge_w, ctx, modc, tile=CTX_LEN, name="merge_ctx", nb=ctx_nb)
            x = _merge(u3, px[7], merge_w, x, modx, tile=merge_tile, name="merge")
        else:
            x = _merge(u3, px[7], merge_w, x, modx, fnw, tile=merge_tile, name="merge_final")
    return x
```

```python
import functools

import jax
import jax.numpy as jnp
from jax import lax
from jax.experimental import pallas as pl
from jax.experimental.pallas import tpu as pltpu

D_MODEL = 1024
CTX_LEN = 256
GRID_W = 64
ROPE_THETA = 10000.0
RMS_EPS = 1e-6
NEG_INF = -1e30
N_BRANCH = 3
WINDOW = 128

HEADS = 8
KV_HEADS = 2
HEAD_DIM = 64
MLA_ROPE = 32
MLA_Q_RANK = 384
MLA_KV_RANK = 256
LOG2E = 1.4426950408889634
MLA_SCALE = (HEAD_DIM + MLA_ROPE) ** -0.5 * LOG2E
GQA_SCALE = HEAD_DIM ** -0.5 * LOG2E
BRANCH_W = HEADS * HEAD_DIM

LANES = 128
N_PAIRS = BRANCH_W // LANES
MLA_QK_W = 2 * LANES

TOK_TILE = 512
MERGE_TILE = 1024
CTX_ROWS_PER_STEP = 4
CTX_PROJ_ROWS_PER_STEP = 2
Q_TILE = 256
BAND = Q_TILE + 2 * WINDOW
KEY_TILE = 256
ATTN_UNROLL = 8

KV_W = MLA_KV_RANK + 5 * LANES
Z_W = 3 * BRANCH_W
G_W = N_BRANCH * D_MODEL
TAB_W = 6 * LANES

VMEM_LIMIT = 56 * 1024 * 1024

_F32 = jnp.float32
_BF16 = jnp.bfloat16


def _params(n_axes):
    return pltpu.CompilerParams(
        dimension_semantics=("arbitrary",) * n_axes, vmem_limit_bytes=VMEM_LIMIT)


def _const_spec(shape):
    nd = len(shape)
    return pl.BlockSpec(shape, lambda *_: (0,) * nd, pipeline_mode=pl.Buffered(1))


def _mod_kernel(c_ref, w_ref, b_ref, o_ref):
    c = c_ref[...]
    s = c / (1.0 + jnp.exp(-c))
    o_ref[0] = jnp.dot(s, w_ref[0], preferred_element_type=_F32,
                       precision=lax.Precision.HIGHEST) + b_ref[0]


def _modulation(cc, ada_w, ada_b):
    depth = ada_w.shape[0]
    rows = cc.shape[0]
    col_tile = D_MODEL
    return pl.pallas_call(
        _mod_kernel,
        grid=(depth, (3 * D_MODEL) // col_tile),
        in_specs=[
            pl.BlockSpec((rows, D_MODEL), lambda l, n: (0, 0)),
            pl.BlockSpec((1, D_MODEL, col_tile), lambda l, n: (l, 0, n)),
            pl.BlockSpec((1, 1, col_tile), lambda l, n: (l, 0, n)),
        ],
        out_specs=pl.BlockSpec((1, rows, col_tile), lambda l, n: (l, 0, n)),
        out_shape=jax.ShapeDtypeStruct((depth, rows, 3 * D_MODEL), _F32),
        compiler_params=_params(2),
        name="modulation",
    )(cc, ada_w, ada_b.reshape(depth, 1, 3 * D_MODEL))


def _rms(x, w):
    ms = jnp.mean(x * x, axis=-1, keepdims=True)
    return x * lax.rsqrt(ms + RMS_EPS) * w


def _head_rms(x, w):
    lane = lax.broadcasted_iota(jnp.int32, x.shape, 1)
    y = x * x
    s = 1
    while s < HEAD_DIM:
        up = pltpu.roll(y, s, 1)
        dn = pltpu.roll(y, LANES - s, 1)
        y = y + jnp.where((lane & s) != 0, up, dn)
        s *= 2
    return x * lax.rsqrt(y * (1.0 / HEAD_DIM) + RMS_EPS) * w


def _rope(x, tab_ref, wide):
    if tab_ref is None:
        return x
    k, r = (0, HEAD_DIM // 4) if wide else (1, MLA_ROPE // 4)
    cos = tab_ref[:, (3 * k) * LANES:(3 * k + 1) * LANES]
    sinm = tab_ref[:, (3 * k + 1) * LANES:(3 * k + 2) * LANES]
    sinp = tab_ref[:, (3 * k + 2) * LANES:(3 * k + 3) * LANES]
    return x * cos + pltpu.roll(x, LANES - r, 1) * sinm + pltpu.roll(x, r, 1) * sinp


def _proj_kernel(*refs, has_rope, with_q):
    refs = list(refs)
    x_ref, mod_ref, nw_ref = refs[:3]
    del refs[:3]
    tab_ref = refs.pop(0) if has_rope else None
    wkv_ref, wq_ref, wuq_ref, wukv_ref, qn_ref, kvn_ref, axq_ref, axk_ref = refs[:8]
    km_ref, vm_ref, kv4_ref = refs[8:11]

    nb, tile, _ = x_ref.shape

    def put(ref, cols, val):
        ref[:, :, cols] = val.reshape(nb, tile, val.shape[-1])

    mod = mod_ref[0]
    x = x_ref[...].reshape(nb * tile, D_MODEL)
    gain = nw_ref[...] * (1.0 + mod[:, D_MODEL:2 * D_MODEL])
    inv = lax.rsqrt(jnp.mean(x * x, axis=-1, keepdims=True) + RMS_EPS)
    h = (x * inv * gain + mod[:, :D_MODEL]).astype(_BF16)

    pkv = jnp.dot(h, wkv_ref[...], preferred_element_type=_F32)
    ckvn = _rms(pkv[:, :MLA_KV_RANK], kvn_ref[...]).astype(_BF16)
    kvm = jnp.dot(ckvn, wukv_ref[...], preferred_element_type=_F32)
    put(vm_ref, slice(None), kvm[:, BRANCH_W:].astype(_BF16))
    o = MLA_KV_RANK
    kr2 = _rope(pkv[:, o:o + LANES], tab_ref, False).astype(_BF16)
    for p in range(N_PAIRS):
        kn = kvm[:, p * LANES:(p + 1) * LANES]
        lo, hi = p * MLA_QK_W, p * MLA_QK_W + LANES
        if has_rope:
            km_ref[0, lo:hi, :] = kn.T.astype(_BF16)
            km_ref[0, hi:hi + LANES, :] = kr2.T
        else:
            put(km_ref, slice(lo, hi), kn.astype(_BF16))
            put(km_ref, slice(hi, hi + LANES), kr2)
    o += LANES
    put(kv4_ref, slice(0, LANES), _rope(pkv[:, o:o + LANES], tab_ref, True).astype(_BF16))
    o += LANES
    put(kv4_ref, slice(LANES, 2 * LANES), pkv[:, o:o + LANES].astype(_BF16))
    o += LANES
    put(kv4_ref, slice(2 * LANES, 3 * LANES),
        _rope(_head_rms(pkv[:, o:o + LANES], axk_ref[...]), tab_ref, True).astype(_BF16))
    o += LANES
    put(kv4_ref, slice(3 * LANES, 4 * LANES), pkv[:, o:o + LANES].astype(_BF16))
    if not with_q:
        return

    qm_ref, qs_ref, qa_ref, z_ref, sg_ref = refs[11:]
    cq = jnp.dot(h, wq_ref[:, :MLA_Q_RANK], preferred_element_type=_F32)
    cqn = _rms(cq, qn_ref[...]).astype(_BF16)
    qm = jnp.dot(cqn, wuq_ref[...], preferred_element_type=_F32)
    for p in range(N_PAIRS):
        lo = p * MLA_QK_W
        put(qm_ref, slice(lo, lo + LANES), (qm[:, lo:lo + LANES] * MLA_SCALE).astype(_BF16))
        put(qm_ref, slice(lo + LANES, lo + MLA_QK_W),
            (_rope(qm[:, lo + LANES:lo + MLA_QK_W], tab_ref, False) * MLA_SCALE).astype(_BF16))
    c0 = MLA_Q_RANK
    sq = jnp.dot(h, wq_ref[:, c0:c0 + BRANCH_W], preferred_element_type=_F32)
    c0 += BRANCH_W
    aq = jnp.dot(h, wq_ref[:, c0:c0 + BRANCH_W], preferred_element_type=_F32)
    c0 += BRANCH_W
    for p in range(N_PAIRS):
        blk = slice(p * LANES, (p + 1) * LANES)
        put(qs_ref, blk, (_rope(sq[:, blk], tab_ref, True) * GQA_SCALE).astype(_BF16))
        aqn = _head_rms(aq[:, blk], axq_ref[...])
        put(qa_ref, blk, (_rope(aqn, tab_ref, True) * GQA_SCALE).astype(_BF16))
    for n in range(Z_W // BRANCH_W):
        z = jnp.dot(h, wq_ref[:, c0:c0 + BRANCH_W], preferred_element_type=_F32)
        put(z_ref, slice(n * BRANCH_W, (n + 1) * BRANCH_W), (z / (1.0 + jnp.exp(-z))).astype(_BF16))
        c0 += BRANCH_W
    for n in range(G_W // BRANCH_W):
        g = jnp.dot(h, wq_ref[:, c0:c0 + BRANCH_W], preferred_element_type=_F32)
        put(sg_ref, slice(n * BRANCH_W, (n + 1) * BRANCH_W), (1.0 / (1.0 + jnp.exp(-g))).astype(_BF16))
        c0 += BRANCH_W


def _project(x, mod, norm_w, tab, weights, *, tile, with_q, name, nb=1):
    B, N, D = x.shape
    assert B % nb == 0 and (nb == 1 or (mod.shape[0] == 1 and tab is None))
    mod_map = (lambda j, b: (b, 0, 0)) if mod.shape[0] == B else (lambda j, b: (0, 0, 0))
    widths = [N_PAIRS * MLA_QK_W, BRANCH_W, 4 * LANES]
    if with_q:
        widths += [N_PAIRS * MLA_QK_W, BRANCH_W, BRANCH_W, Z_W, G_W]
    in_specs = [pl.BlockSpec((nb, tile, D), lambda j, b: (b, j, 0)),
                pl.BlockSpec((1, 1, 3 * D), mod_map),
                _const_spec(norm_w.shape)]
    args = [x, mod, norm_w]
    if tab is not None:
        in_specs.append(pl.BlockSpec((tile, TAB_W), lambda j, b: (j, 0)))
        args.append(tab)
    in_specs += [_const_spec(w.shape) for w in weights]
    out_specs = [pl.BlockSpec((nb, tile, w), lambda j, b: (b, j, 0)) for w in widths]
    out_shape = [jax.ShapeDtypeStruct((B, N, w), _BF16) for w in widths]
    if tab is not None:
        out_specs[0] = pl.BlockSpec((1, widths[0], tile), lambda j, b: (b, 0, j))
        out_shape[0] = jax.ShapeDtypeStruct((B, widths[0], N), _BF16)
    return pl.pallas_call(
        functools.partial(_proj_kernel, has_rope=tab is not None, with_q=with_q),
        grid=(N // tile, B // nb),
        in_specs=in_specs,
        out_specs=out_specs,
        out_shape=out_shape,
        compiler_params=_params(2),
        name=name,
    )(*args, *weights)


def _scores(q, k):
    return lax.dot_general(q, k, (((1,), (1,)), ((), ())), preferred_element_type=_F32)


def _stack_heads(q, mla):
    lane = lax.broadcasted_iota(jnp.int32, q.shape, 1)
    is_a = lane < HEAD_DIM
    if mla:
        is_a = is_a | ((lane >= LANES) & (lane < LANES + MLA_ROPE))
    zero = jnp.zeros_like(q)
    return jnp.concatenate([jnp.where(is_a, q, zero), jnp.where(is_a, zero, q)], axis=0)


def _fill_v1(v1_scr, v_ref, bi, n_blocks):
    n = v_ref.shape[1]
    for j in range(n_blocks):
        v1_scr[j, :, :LANES] = v_ref[bi, :, j * LANES:(j + 1) * LANES]
        v1_scr[j, :, LANES:] = jnp.ones((n, LANES), _BF16)


def _online_softmax_pv(q2, key_tiles, sink=None):
    m = acc = None
    for k, v1, bias, k_is_transposed in key_tiles:
        s = jnp.dot(q2, k, preferred_element_type=_F32) if k_is_transposed else _scores(q2, k)
        if bias is not None:
            s = s + bias
        m_new = jnp.max(s, axis=-1, keepdims=True)
        if m is not None:
            m_new = jnp.maximum(m, m_new)
        elif sink is not None:
            m_new = jnp.maximum(sink, m_new)
        pv = jnp.dot(jnp.exp2(s - m_new).astype(_BF16), v1, preferred_element_type=_F32)
        acc = pv if acc is None else jnp.exp2(m - m_new) * acc + pv
        m = m_new
    den = acc[:, LANES:]
    if sink is not None:
        den = den + jnp.exp2(sink - m)
    return acc[:, :LANES] / den


def _gate_store(o_ref, z_ref, bi, rows, p, r):
    n = r.shape[0] // 2
    lane = lax.broadcasted_iota(jnp.int32, (n, LANES), 1)
    o = jnp.where(lane < HEAD_DIM, r[:n], r[n:])
    blk = slice(p * LANES, (p + 1) * LANES)
    o_ref[bi, rows, blk] = (o * z_ref[bi, rows, blk].astype(_F32)).astype(_BF16)


def _tile_rows(tile):
    return pl.ds(pl.multiple_of(tile * Q_TILE, Q_TILE), Q_TILE)


def _attn_kernel(*refs, mla, window, latent_keys):
    refs = list(refs)
    sink_ref = refs.pop(0) if window else None
    q_ref, z_ref, kc_ref, vc_ref = refs[:4]
    k_ref, v_ref = refs[4:6] if latent_keys else (None, None)
    o_ref, vc1_scr = refs[-2 - latent_keys], refs[-1 - latent_keys]
    v1_scr = refs[-1] if latent_keys else None
    for bi in range(q_ref.shape[0]):
        _attn_row(bi, sink_ref, q_ref, z_ref, kc_ref, vc_ref, k_ref, v_ref, o_ref, vc1_scr, v1_scr,
                  mla=mla, window=window)


def _attn_row(bi, sink_ref, q_ref, z_ref, kc_ref, vc_ref, k_ref, v_ref, o_ref, vc1_scr, v1_scr, *, mla, window):
    latent_keys = k_ref is not None
    n_tiles = q_ref.shape[1] // Q_TILE
    wq = MLA_QK_W if mla else LANES
    n_blocks = N_PAIRS if mla else 1
    vc1_scr = vc1_scr.at[bi]
    _fill_v1(vc1_scr, vc_ref, bi, n_blocks)
    if latent_keys:
        _fill_v1(v1_scr, v_ref, bi, n_blocks)
        n_lat = v_ref.shape[1]

    def ctx_tile(p):
        kb = slice(p * wq, (p + 1) * wq) if mla else slice(None)
        return (kc_ref[bi, :, kb], vc1_scr[p if mla else 0], None, False)

    def latent_tile(p, start, bias):
        keys = pl.ds(start, KEY_TILE)
        if mla:
            return (k_ref[bi, p * wq:(p + 1) * wq, keys], v1_scr[p, keys, :], bias, True)
        return (k_ref[bi, keys, :], v1_scr[0, keys, :], bias, False)

    def pair_sink(p):
        row = lax.broadcasted_iota(jnp.int32, (2 * Q_TILE, 1), 0)
        return jnp.where(row < Q_TILE, sink_ref[p], sink_ref[N_PAIRS + p]) * LOG2E

    def step(t, carry):
        if window and latent_keys:
            q0 = t * Q_TILE
            k0 = jnp.clip(q0 - WINDOW, 0, n_lat - BAND)
            shape = (2 * Q_TILE, KEY_TILE)
            col = lax.broadcasted_iota(jnp.int32, shape, 1)
            row = lax.broadcasted_iota(jnp.int32, shape, 0) & (Q_TILE - 1)
            lat = [(pl.multiple_of(k0 + b0, LANES),
                    jnp.where(jnp.abs((col + (k0 + b0)) - (row + q0)) <= WINDOW, 0.0, NEG_INF))
                   for b0 in range(0, BAND, KEY_TILE)]
        elif latent_keys:
            lat = [(k0, None) for k0 in range(0, n_lat, KEY_TILE)]
        else:
            lat = []
        rows = _tile_rows(t)
        for p in range(N_PAIRS):
            tiles = [latent_tile(p, start, bias) for start, bias in lat] + [ctx_tile(p)]
            q2 = _stack_heads(q_ref[bi, rows, p * wq:(p + 1) * wq], mla)
            r = _online_softmax_pv(q2, tiles, pair_sink(p) if window else None)
            _gate_store(o_ref, z_ref, bi, rows, p, r)
        return carry

    lax.fori_loop(0, n_tiles, step, 0, unroll=min(n_tiles, ATTN_UNROLL))


def _attention(q, z, z_blk, kc, vc, k=None, v=None, *, mla=False, sink=None, name, nb=1):
    B, nq, _ = q.shape
    latent_keys = k is not None

    def kv_spec(pair):
        a, blk = pair
        if mla:
            return pl.BlockSpec((nb,) + a.shape[1:], lambda b: (b, 0, 0))
        return pl.BlockSpec((nb, a.shape[1], LANES), lambda b: (b, 0, blk))

    args, in_specs = [], []
    if sink is not None:
        args.append(sink)
        in_specs.append(pl.BlockSpec(memory_space=pltpu.SMEM))
    args += [q, z, kc[0], vc[0]]
    in_specs += [pl.BlockSpec((nb,) + q.shape[1:], lambda b: (b, 0, 0)),
                 pl.BlockSpec((nb, nq, BRANCH_W), lambda b: (b, 0, z_blk)), kv_spec(kc), kv_spec(vc)]
    n_blocks = N_PAIRS if mla else 1
    scratch = [pltpu.VMEM((nb, n_blocks, CTX_LEN, 2 * LANES), _BF16)]
    if latent_keys:
        args += [k[0], v[0]]
        in_specs += [kv_spec(k), kv_spec(v)]
        scratch.append(pltpu.VMEM((n_blocks, v[0].shape[1], 2 * LANES), _BF16))
    return pl.pallas_call(
        functools.partial(_attn_kernel, mla=mla, window=sink is not None, latent_keys=latent_keys),
        grid=(B // nb,),
        in_specs=in_specs,
        out_specs=pl.BlockSpec((nb, nq, BRANCH_W), lambda b: (b, 0, 0)),
        out_shape=jax.ShapeDtypeStruct((B, nq, BRANCH_W), _BF16),
        scratch_shapes=scratch,
        compiler_params=_params(1),
        name=name,
    )(*args)


def _merge_kernel(um_ref, us_ref, ua_ref, sg_ref, wom_ref, wos_ref, woa_ref, wout_ref,
                  x_ref, mod_ref, *rest, final_norm):
    nb, tile, _ = x_ref.shape
    rows = nb * tile
    y = None
    for n, (u_ref, w_ref) in enumerate(((um_ref, wom_ref), (us_ref, wos_ref), (ua_ref, woa_ref))):
        yb = jnp.dot(u_ref[...].reshape(rows, BRANCH_W), w_ref[...], preferred_element_type=_F32)
        yb = yb * sg_ref[:, :, n * D_MODEL:(n + 1) * D_MODEL].reshape(rows, D_MODEL).astype(_F32)
        y = yb if y is None else y + yb
    out = jnp.dot(y.astype(_BF16), wout_ref[...], preferred_element_type=_F32)
    xn = x_ref[...].reshape(rows, D_MODEL) + mod_ref[0][:, 2 * D_MODEL:] * out
    if final_norm:
        fnw_ref, o_ref = rest
        o_ref[...] = _rms(xn, fnw_ref[...]).reshape(nb, tile, D_MODEL)
    else:
        (o_ref,) = rest
        o_ref[...] = xn.reshape(nb, tile, D_MODEL)


def _merge(us, sg, weights, x, mod, fnw=None, *, tile, name, nb=1):
    B, N, D = x.shape
    assert B % nb == 0 and (nb == 1 or mod.shape[0] == 1)
    mod_map = (lambda b, j: (b, 0, 0)) if mod.shape[0] == B else (lambda b, j: (0, 0, 0))
    tok = lambda w: pl.BlockSpec((nb, tile, w), lambda b, j: (b, j, 0))
    in_specs = ([tok(BRANCH_W)] * 3 + [tok(G_W)] + [_const_spec(w.shape) for w in weights]
                + [tok(D), pl.BlockSpec((1, 1, 3 * D), mod_map)])
    args = [*us, sg, *weights, x, mod]
    if fnw is not None:
        in_specs.append(_const_spec(fnw.shape))
        args.append(fnw)
    return pl.pallas_call(
        functools.partial(_merge_kernel, final_norm=fnw is not None),
        grid=(B // nb, N // tile),
        in_specs=in_specs,
        out_specs=tok(D),
        out_shape=jax.ShapeDtypeStruct((B, N, D), _F32),
        compiler_params=_params(2),
        name=name,
    )(*args)


def _axial_tables(d, n_lat):
    h = d // 2
    t = jnp.arange(n_lat, dtype=jnp.int32)
    pos_row = (t // GRID_W).astype(_F32)
    pos_col = (t % GRID_W).astype(_F32)
    freqs = ROPE_THETA ** (-jnp.arange(0, h, 2, dtype=_F32) / h)
    ang_r = pos_row[:, None] * freqs[None, :]
    ang_c = pos_col[:, None] * freqs[None, :]
    cr, sr, cc, sc = jnp.cos(ang_r), jnp.sin(ang_r), jnp.cos(ang_c), jnp.sin(ang_c)
    z = jnp.zeros_like(sr)
    cos = jnp.concatenate([cr, cr, cc, cc], axis=-1)
    sinm = jnp.concatenate([-sr, z, -sc, z], axis=-1)
    sinp = jnp.concatenate([z, sr, z, sc], axis=-1)
    return cos, sinm, sinp


def _rope_tables(n_lat):
    set64 = [jnp.tile(t, (1, 2)) for t in _axial_tables(HEAD_DIM, n_lat)]
    pad = LANES - 2 * MLA_ROPE
    set32 = [jnp.concatenate([t, t, jnp.full((n_lat, pad), f, _F32)], axis=1)
             for t, f in zip(_axial_tables(MLA_ROPE, n_lat), (1.0, 0.0, 0.0))]
    return jnp.concatenate(set64 + set32, axis=1)


def _permute_heads(w, axis):
    shape = w.shape
    w = w.reshape(shape[:axis] + (KV_HEADS, HEADS // KV_HEADS, HEAD_DIM) + shape[axis + 1:])
    return jnp.swapaxes(w, axis, axis + 1).reshape(shape)


def _layer_weights(w_in, w_uq, w_ukv, w_o_swa, w_o_ax):
    D = w_in.shape[0]
    offs = {}
    o = 0
    for name, w in (("ckv", MLA_KV_RANK), ("kr", MLA_ROPE), ("sk", LANES), ("sv", LANES),
                    ("ak", LANES), ("av", LANES), ("cq", MLA_Q_RANK), ("sq", BRANCH_W),
                    ("aq", BRANCH_W), ("zm", BRANCH_W), ("zs", BRANCH_W), ("za", BRANCH_W),
                    ("g", G_W)):
        offs[name] = w_in[:, o:o + w]
        o += w
    kr = offs["kr"]
    kr2 = jnp.concatenate([kr, kr, jnp.zeros((D, LANES - 2 * MLA_ROPE), w_in.dtype)], axis=1)
    wkv = jnp.concatenate([offs["ckv"], kr2, offs["sk"], offs["sv"], offs["ak"], offs["av"]], axis=1)
    wq = jnp.concatenate([offs["cq"], _permute_heads(offs["sq"], 1), _permute_heads(offs["aq"], 1),
                          offs["zm"], _permute_heads(offs["zs"], 1), _permute_heads(offs["za"], 1),
                          offs["g"]], axis=1)
    uq = w_uq.reshape(MLA_Q_RANK, N_PAIRS, 2, HEAD_DIM + MLA_ROPE)
    nope = uq[..., :HEAD_DIM].reshape(MLA_Q_RANK, N_PAIRS, 2 * HEAD_DIM)
    rope = uq[..., HEAD_DIM:].reshape(MLA_Q_RANK, N_PAIRS, 2 * MLA_ROPE)
    pad = jnp.zeros((MLA_Q_RANK, N_PAIRS, MLA_QK_W - 2 * HEAD_DIM - 2 * MLA_ROPE), w_uq.dtype)
    wuq = jnp.concatenate([nope, rope, pad], axis=-1).reshape(MLA_Q_RANK, N_PAIRS * MLA_QK_W)
    ukv = w_ukv.reshape(MLA_KV_RANK, HEADS, 2, HEAD_DIM)
    wukv = jnp.concatenate([ukv[:, :, 0].reshape(MLA_KV_RANK, BRANCH_W),
                            ukv[:, :, 1].reshape(MLA_KV_RANK, BRANCH_W)], axis=1)
    bf = lambda a: a.astype(_BF16)
    return (bf(wkv), bf(wq), bf(wuq), bf(wukv),
            bf(_permute_heads(w_o_swa, 0)), bf(_permute_heads(w_o_ax, 0)))


def _attend_all(sink, pq, pc, px=None, *, tag, nb=1):
    qm, qs, qa, z = pq[3:7]
    lat = lambda i, blk: None if px is None else (px[i], blk)
    return (_attention(qm, z, 0, (pc[0], 0), (pc[1], 0), lat(0, 0), lat(1, 0), mla=True, name="attn_mla" + tag, nb=nb),
            _attention(qs, z, 1, (pc[2], 0), (pc[2], 1), lat(2, 0), lat(2, 1), sink=sink, name="attn_window" + tag, nb=nb),
            _attention(qa, z, 2, (pc[2], 2), (pc[2], 3), lat(2, 2), lat(2, 3), name="attn_axial" + tag, nb=nb))


def kernel(x, c, ctx, c_ctx, ada_w, ada_b, norm_w, w_in, mla_q_norm, mla_w_uq, mla_kv_norm, mla_w_ukv,
           swa_sink, ax_q_norm, ax_k_norm, w_o_mla, w_o_swa, w_o_ax, w_out, final_norm_w):
    B, T, D = x.shape
    depth = w_in.shape[0]
    merge_tile = min(MERGE_TILE, T)
    ctx_nb = CTX_ROWS_PER_STEP if B % CTX_ROWS_PER_STEP == 0 else 1
    ctx_proj_nb = CTX_PROJ_ROWS_PER_STEP if B % CTX_PROJ_ROWS_PER_STEP == 0 else 1
    assert (D, ctx.shape[1]) == (D_MODEL, CTX_LEN) and T % TOK_TILE == 0 and T % merge_tile == 0 and T >= BAND

    mod_rows = -(-(B + 1) // 8) * 8
    cc = jnp.concatenate([c, c_ctx[None, :], jnp.zeros((mod_rows - B - 1, D), c.dtype)], axis=0)
    mod = _modulation(cc, ada_w, ada_b)
    tab = _rope_tables(T)
    fnw = final_norm_w.reshape(1, D)
    two_heads = lambda w: jnp.tile(w, 2).reshape(1, LANES)

    for l in range(depth):
        update_ctx = l < depth - 1
        wkv, wq, wuq, wukv, wos, woa = _layer_weights(w_in[l], mla_w_uq[l], mla_w_ukv[l], w_o_swa[l], w_o_ax[l])
        proj_w = (wkv, wq, wuq, wukv, mla_q_norm[l].reshape(1, -1), mla_kv_norm[l].reshape(1, -1),
                  two_heads(ax_q_norm[l]), two_heads(ax_k_norm[l]))
        merge_w = (w_o_mla[l].astype(_BF16), wos, woa, w_out[l].astype(_BF16))
        nw = norm_w[l].reshape(1, D)
        modx = mod[l, :B].reshape(B, 1, 3 * D)
        modc = mod[l, B:B + 1].reshape(1, 1, 3 * D)

        pc = _project(ctx, modc, nw, None, proj_w, tile=CTX_LEN, with_q=update_ctx, name="project_ctx", nb=ctx_proj_nb)
        px = _project(x, modx, nw, tab, proj_w, tile=TOK_TILE, with_q=True, name="project")
        u3 = _attend_all(swa_sink[l], px, pc, px, tag="")
        if update_ctx:
            uc3 = _attend_all(swa_sink[l], pc, pc, tag="_ctx", nb=ctx_nb)
            ctx = _merge(uc3, pc[7], merge_w, ctx, modc, tile=CTX_LEN, name="merge_ctx", nb=ctx_nb)
            x = _merge(u3, px[7], merge_w, x, modx, tile=merge_tile, name="merge")
        else:
            x = _merge(u3, px[7], merge_w, x, modx, fnw, tile=merge_tile, name="merge_final")
    return x
```

```python
import functools

import jax
import jax.numpy as jnp
from jax import lax
from jax.experimental import pallas as pl
from jax.experimental.pallas import tpu as pltpu

D_MODEL = 1024
CTX_LEN = 256
GRID_W = 64
ROPE_THETA = 10000.0
RMS_EPS = 1e-6
NEG_INF = -1e30
N_BRANCH = 3
WINDOW = 128

HEADS = 8
KV_HEADS = 2
HEAD_DIM = 64
MLA_ROPE = 32
MLA_Q_RANK = 384
MLA_KV_RANK = 256
LOG2E = 1.4426950408889634
MLA_SCALE = (HEAD_DIM + MLA_ROPE) ** -0.5 * LOG2E
GQA_SCALE = HEAD_DIM ** -0.5 * LOG2E
BRANCH_W = HEADS * HEAD_DIM

LANES = 128
N_PAIRS = BRANCH_W // LANES
MLA_QK_W = 2 * LANES

TOK_TILE = 512
MERGE_TILE = 1024
CTX_ROWS_PER_STEP = 4
CTX_PROJ_ROWS_PER_STEP = 2
Q_TILE = 256
BAND = Q_TILE + 2 * WINDOW
KEY_TILE = 256
ATTN_UNROLL = 4

KV_W = MLA_KV_RANK + 5 * LANES
Z_W = 3 * BRANCH_W
G_W = N_BRANCH * D_MODEL
TAB_W = 6 * LANES

VMEM_LIMIT = 56 * 1024 * 1024

_F32 = jnp.float32
_BF16 = jnp.bfloat16


def _params(n_axes):
    return pltpu.CompilerParams(
        dimension_semantics=("arbitrary",) * n_axes, vmem_limit_bytes=VMEM_LIMIT)


def _const_spec(shape):
    nd = len(shape)
    return pl.BlockSpec(shape, lambda *_: (0,) * nd, pipeline_mode=pl.Buffered(1))


def _mod_kernel(c_ref, w_ref, b_ref, o_ref):
    c = c_ref[...]
    s = c / (1.0 + jnp.exp(-c))
    o_ref[0] = jnp.dot(s, w_ref[0], preferred_element_type=_F32,
                       precision=lax.Precision.HIGHEST) + b_ref[0]


def _modulation(cc, ada_w, ada_b):
    depth = ada_w.shape[0]
    rows = cc.shape[0]
    col_tile = D_MODEL
    return pl.pallas_call(
        _mod_kernel,
        grid=(depth, (3 * D_MODEL) // col_tile),
        in_specs=[
            pl.BlockSpec((rows, D_MODEL), lambda l, n: (0, 0)),
            pl.BlockSpec((1, D_MODEL, col_tile), lambda l, n: (l, 0, n)),
            pl.BlockSpec((1, 1, col_tile), lambda l, n: (l, 0, n)),
        ],
        out_specs=pl.BlockSpec((1, rows, col_tile), lambda l, n: (l, 0, n)),
        out_shape=jax.ShapeDtypeStruct((depth, rows, 3 * D_MODEL), _F32),
        compiler_params=_params(2),
        name="modulation",
    )(cc, ada_w, ada_b.reshape(depth, 1, 3 * D_MODEL))


def _rms(x, w):
    ms = jnp.mean(x * x, axis=-1, keepdims=True)
    return x * lax.rsqrt(ms + RMS_EPS) * w


def _head_rms(x, w):
    lane = lax.broadcasted_iota(jnp.int32, x.shape, 1)
    y = x * x
    s = 1
    while s < HEAD_DIM:
        up = pltpu.roll(y, s, 1)
        dn = pltpu.roll(y, LANES - s, 1)
        y = y + jnp.where((lane & s) != 0, up, dn)
        s *= 2
    return x * lax.rsqrt(y * (1.0 / HEAD_DIM) + RMS_EPS) * w


def _rope(x, tab_ref, wide):
    if tab_ref is None:
        return x
    k, r = (0, HEAD_DIM // 4) if wide else (1, MLA_ROPE // 4)
    cos = tab_ref[:, (3 * k) * LANES:(3 * k + 1) * LANES]
    sinm = tab_ref[:, (3 * k + 1) * LANES:(3 * k + 2) * LANES]
    sinp = tab_ref[:, (3 * k + 2) * LANES:(3 * k + 3) * LANES]
    return x * cos + pltpu.roll(x, LANES - r, 1) * sinm + pltpu.roll(x, r, 1) * sinp


def _proj_kernel(*refs, has_rope, with_q, per_row_mod):
    refs = list(refs)
    x_ref, mod_ref, nw_ref = refs[:3]
    del refs[:3]
    tab_ref = refs.pop(0) if has_rope else None
    wkv_ref, wq_ref, wuq_ref, wukv_ref, qn_ref, kvn_ref, axq_ref, axk_ref = refs[:8]
    km_ref, vkv_ref = refs[8:10]

    nb, tile, _ = x_ref.shape

    def put(ref, cols, val):
        ref[:, :, cols] = val.reshape(nb, tile, val.shape[-1])

    mod = mod_ref[pl.program_id(1) if per_row_mod else 0]
    x = x_ref[...].reshape(nb * tile, D_MODEL)
    gain = nw_ref[...] * (1.0 + mod[:, D_MODEL:2 * D_MODEL])
    inv = lax.rsqrt(jnp.mean(x * x, axis=-1, keepdims=True) + RMS_EPS)
    h = (x * inv * gain + mod[:, :D_MODEL]).astype(_BF16)

    pkv = jnp.dot(h, wkv_ref[...], preferred_element_type=_F32)
    ckvn = _rms(pkv[:, :MLA_KV_RANK], kvn_ref[...]).astype(_BF16)
    kvm = jnp.dot(ckvn, wukv_ref[...], preferred_element_type=_F32)
    put(vkv_ref, slice(0, BRANCH_W), kvm[:, BRANCH_W:].astype(_BF16))
    o = MLA_KV_RANK
    kr2 = _rope(pkv[:, o:o + LANES], tab_ref, False).astype(_BF16)
    for p in range(N_PAIRS):
        kn = kvm[:, p * LANES:(p + 1) * LANES]
        lo, hi = p * MLA_QK_W, p * MLA_QK_W + LANES
        if has_rope:
            km_ref[0, lo:hi, :] = kn.T.astype(_BF16)
            km_ref[0, hi:hi + LANES, :] = kr2.T
        else:
            put(km_ref, slice(lo, hi), kn.astype(_BF16))
            put(km_ref, slice(hi, hi + LANES), kr2)
    o += LANES
    put(vkv_ref, slice(BRANCH_W, BRANCH_W + LANES), _rope(pkv[:, o:o + LANES], tab_ref, True).astype(_BF16))
    o += LANES
    put(vkv_ref, slice(BRANCH_W + LANES, BRANCH_W + 2 * LANES), pkv[:, o:o + LANES].astype(_BF16))
    o += LANES
    put(vkv_ref, slice(BRANCH_W + 2 * LANES, BRANCH_W + 3 * LANES),
        _rope(_head_rms(pkv[:, o:o + LANES], axk_ref[...]), tab_ref, True).astype(_BF16))
    o += LANES
    put(vkv_ref, slice(BRANCH_W + 3 * LANES, BRANCH_W + 4 * LANES), pkv[:, o:o + LANES].astype(_BF16))
    if not with_q:
        return

    q_ref, gz_ref = refs[10:]
    cq = jnp.dot(h, wq_ref[:, :MLA_Q_RANK], preferred_element_type=_F32)
    cqn = _rms(cq, qn_ref[...]).astype(_BF16)
    qm = jnp.dot(cqn, wuq_ref[...], preferred_element_type=_F32)
    for p in range(N_PAIRS):
        lo = p * MLA_QK_W
        put(q_ref, slice(lo, lo + LANES), (qm[:, lo:lo + LANES] * MLA_SCALE).astype(_BF16))
        put(q_ref, slice(lo + LANES, lo + MLA_QK_W),
            (_rope(qm[:, lo + LANES:lo + MLA_QK_W], tab_ref, False) * MLA_SCALE).astype(_BF16))
    c0 = MLA_Q_RANK
    sq = jnp.dot(h, wq_ref[:, c0:c0 + BRANCH_W], preferred_element_type=_F32)
    c0 += BRANCH_W
    aq = jnp.dot(h, wq_ref[:, c0:c0 + BRANCH_W], preferred_element_type=_F32)
    c0 += BRANCH_W
    for p in range(N_PAIRS):
        blk = slice(p * LANES, (p + 1) * LANES)
        qs0 = N_PAIRS * MLA_QK_W + p * LANES
        put(q_ref, slice(qs0, qs0 + LANES), (_rope(sq[:, blk], tab_ref, True) * GQA_SCALE).astype(_BF16))
        aqn = _head_rms(aq[:, blk], axq_ref[...])
        put(q_ref, slice(qs0 + BRANCH_W, qs0 + BRANCH_W + LANES), (_rope(aqn, tab_ref, True) * GQA_SCALE).astype(_BF16))
    for n in range(Z_W // BRANCH_W):
        z = jnp.dot(h, wq_ref[:, c0:c0 + BRANCH_W], preferred_element_type=_F32)
        put(gz_ref, slice(G_W + n * BRANCH_W, G_W + (n + 1) * BRANCH_W), (z / (1.0 + jnp.exp(-z))).astype(_BF16))
        c0 += BRANCH_W
    for n in range(G_W // BRANCH_W):
        g = jnp.dot(h, wq_ref[:, c0:c0 + BRANCH_W], preferred_element_type=_F32)
        put(gz_ref, slice(n * BRANCH_W, (n + 1) * BRANCH_W), (1.0 / (1.0 + jnp.exp(-g))).astype(_BF16))
        c0 += BRANCH_W


def _project(x, mod, norm_w, tab, weights, *, tile, with_q, name, nb=1):
    B, N, D = x.shape
    assert B % nb == 0 and (nb == 1 or (mod.shape[0] == 1 and tab is None))
    widths = [N_PAIRS * MLA_QK_W, BRANCH_W + 4 * LANES]
    if with_q:
        widths += [N_PAIRS * MLA_QK_W + 2 * BRANCH_W, G_W + Z_W]
    in_specs = [pl.BlockSpec((nb, tile, D), lambda j, b: (b, j, 0)),
                _const_spec(mod.shape),
                _const_spec(norm_w.shape)]
    args = [x, mod, norm_w]
    if tab is not None:
        in_specs.append(pl.BlockSpec((tile, TAB_W), lambda j, b: (j, 0)))
        args.append(tab)
    in_specs += [_const_spec(w.shape) for w in weights]
    out_specs = [pl.BlockSpec((nb, tile, w), lambda j, b: (b, j, 0)) for w in widths]
    out_shape = [jax.ShapeDtypeStruct((B, N, w), _BF16) for w in widths]
    if tab is not None:
        out_specs[0] = pl.BlockSpec((1, widths[0], tile), lambda j, b: (b, 0, j))
        out_shape[0] = jax.ShapeDtypeStruct((B, widths[0], N), _BF16)
    return pl.pallas_call(
        functools.partial(_proj_kernel, has_rope=tab is not None, with_q=with_q, per_row_mod=mod.shape[0] == B),
        grid=(N // tile, B // nb),
        in_specs=in_specs,
        out_specs=out_specs,
        out_shape=out_shape,
        compiler_params=_params(2),
        name=name,
    )(*args, *weights)


def _scores(q, k):
    return lax.dot_general(q, k, (((1,), (1,)), ((), ())), preferred_element_type=_F32)


def _stack_heads(q, mla):
    lane = lax.broadcasted_iota(jnp.int32, q.shape, 1)
    is_a = lane < HEAD_DIM
    if mla:
        is_a = is_a | ((lane >= LANES) & (lane < LANES + MLA_ROPE))
    zero = jnp.zeros_like(q)
    return jnp.concatenate([jnp.where(is_a, q, zero), jnp.where(is_a, zero, q)], axis=0)


def _fill_v1(v1_scr, v_ref, bi, n_blocks):
    n = v_ref.shape[1]
    for j in range(n_blocks):
        v1_scr[j, :, :LANES] = v_ref[bi, :, j * LANES:(j + 1) * LANES]
        v1_scr[j, :, LANES:] = jnp.ones((n, LANES), _BF16)


def _online_softmax_pv(q2, key_tiles, sink=None):
    m = acc = None
    for k, v1, bias, k_is_transposed in key_tiles:
        s = jnp.dot(q2, k, preferred_element_type=_F32) if k_is_transposed else _scores(q2, k)
        if bias is not None:
            s = s + bias
        m_new = jnp.max(s, axis=-1, keepdims=True)
        if m is not None:
            m_new = jnp.maximum(m, m_new)
        elif sink is not None:
            m_new = jnp.maximum(sink, m_new)
        pv = jnp.dot(jnp.exp2(s - m_new).astype(_BF16), v1, preferred_element_type=_F32)
        acc = pv if acc is None else jnp.exp2(m - m_new) * acc + pv
        m = m_new
    den = acc[:, LANES:]
    if sink is not None:
        den = den + jnp.exp2(sink - m)
    return acc[:, :LANES] / den


def _gate_store(o_ref, z_ref, bi, rows, p, r):
    n = r.shape[0] // 2
    lane = lax.broadcasted_iota(jnp.int32, (n, LANES), 1)
    o = jnp.where(lane < HEAD_DIM, r[:n], r[n:])
    blk = slice(p * LANES, (p + 1) * LANES)
    o_ref[bi, rows, blk] = (o * z_ref[bi, rows, blk].astype(_F32)).astype(_BF16)


def _tile_rows(tile):
    return pl.ds(pl.multiple_of(tile * Q_TILE, Q_TILE), Q_TILE)


def _attn_kernel(*refs, mla, window, latent_keys):
    refs = list(refs)
    sink_ref = refs.pop(0) if window else None
    q_ref, z_ref, kc_ref, vc_ref = refs[:4]
    k_ref, v_ref = refs[4:6] if latent_keys else (None, None)
    o_ref, vc1_scr = refs[-2 - latent_keys], refs[-1 - latent_keys]
    v1_scr = refs[-1] if latent_keys else None
    for bi in range(q_ref.shape[0]):
        _attn_row(bi, sink_ref, q_ref, z_ref, kc_ref, vc_ref, k_ref, v_ref, o_ref, vc1_scr, v1_scr,
                  mla=mla, window=window)


def _attn_row(bi, sink_ref, q_ref, z_ref, kc_ref, vc_ref, k_ref, v_ref, o_ref, vc1_scr, v1_scr, *, mla, window):
    latent_keys = k_ref is not None
    n_tiles = q_ref.shape[1] // Q_TILE
    wq = MLA_QK_W if mla else LANES
    n_blocks = N_PAIRS if mla else 1
    vc1_scr = vc1_scr.at[bi]
    _fill_v1(vc1_scr, vc_ref, bi, n_blocks)
    if latent_keys:
        _fill_v1(v1_scr, v_ref, bi, n_blocks)
        n_lat = v_ref.shape[1]

    def ctx_tile(p):
        kb = slice(p * wq, (p + 1) * wq) if mla else slice(None)
        return (kc_ref[bi, :, kb], vc1_scr[p if mla else 0], None, False)

    def latent_tile(p, start, bias):
        keys = pl.ds(start, KEY_TILE)
        if mla:
            return (k_ref[bi, p * wq:(p + 1) * wq, keys], v1_scr[p, keys, :], bias, True)
        return (k_ref[bi, keys, :], v1_scr[0, keys, :], bias, False)

    def pair_sink(p):
        row = lax.broadcasted_iota(jnp.int32, (2 * Q_TILE, 1), 0)
        return jnp.where(row < Q_TILE, sink_ref[p], sink_ref[N_PAIRS + p]) * LOG2E

    def step(t, carry):
        if window and latent_keys:
            q0 = t * Q_TILE
            k0 = jnp.clip(q0 - WINDOW, 0, n_lat - BAND)
            shape = (2 * Q_TILE, KEY_TILE)
            col = lax.broadcasted_iota(jnp.int32, shape, 1)
            row = lax.broadcasted_iota(jnp.int32, shape, 0) & (Q_TILE - 1)
            lat = [(pl.multiple_of(k0 + b0, LANES),
                    jnp.where(jnp.abs((col + (k0 + b0)) - (row + q0)) <= WINDOW, 0.0, NEG_INF))
                   for b0 in range(0, BAND, KEY_TILE)]
        elif latent_keys:
            lat = [(k0, None) for k0 in range(0, n_lat, KEY_TILE)]
        else:
            lat = []
        rows = _tile_rows(t)
        for p in range(N_PAIRS):
            tiles = [latent_tile(p, start, bias) for start, bias in lat] + [ctx_tile(p)]
            q2 = _stack_heads(q_ref[bi, rows, p * wq:(p + 1) * wq], mla)
            r = _online_softmax_pv(q2, tiles, pair_sink(p) if window else None)
            _gate_store(o_ref, z_ref, bi, rows, p, r)
        return carry

    lax.fori_loop(0, n_tiles, step, 0, unroll=min(n_tiles, ATTN_UNROLL))


def _attention(q, z, kc, vc, k=None, v=None, *, mla=False, sink=None, name, nb=1):
    B, nq, _ = q[0].shape
    latent_keys = k is not None

    def view(operand):
        a, width, blk = operand
        if width is None:
            return pl.BlockSpec((nb,) + a.shape[1:], lambda b: (b, 0, 0))
        return pl.BlockSpec((nb, a.shape[1], width), lambda b: (b, 0, blk))

    operands = [q, z, kc, vc] + ([k, v] if latent_keys else [])
    args = [o[0] for o in operands]
    in_specs = [view(o) for o in operands]
    if sink is not None:
        args.insert(0, sink)
        in_specs.insert(0, pl.BlockSpec(memory_space=pltpu.SMEM))
    n_blocks = N_PAIRS if mla else 1
    scratch = [pltpu.VMEM((nb, n_blocks, CTX_LEN, 2 * LANES), _BF16)]
    if latent_keys:
        scratch.append(pltpu.VMEM((n_blocks, v[0].shape[1], 2 * LANES), _BF16))
    return pl.pallas_call(
        functools.partial(_attn_kernel, mla=mla, window=sink is not None, latent_keys=latent_keys),
        grid=(B // nb,),
        in_specs=in_specs,
        out_specs=pl.BlockSpec((nb, nq, BRANCH_W), lambda b: (b, 0, 0)),
        out_shape=jax.ShapeDtypeStruct((B, nq, BRANCH_W), _BF16),
        scratch_shapes=scratch,
        compiler_params=_params(1),
        name=name,
    )(*args)


def _merge_kernel(um_ref, us_ref, ua_ref, sg_ref, wom_ref, wos_ref, woa_ref, wout_ref,
                  x_ref, mod_ref, *rest, final_norm, per_row_mod):
    nb, tile, _ = x_ref.shape
    rows = nb * tile
    y = None
    for n, (u_ref, w_ref) in enumerate(((um_ref, wom_ref), (us_ref, wos_ref), (ua_ref, woa_ref))):
        yb = jnp.dot(u_ref[...].reshape(rows, BRANCH_W), w_ref[...], preferred_element_type=_F32)
        yb = yb * sg_ref[:, :, n * D_MODEL:(n + 1) * D_MODEL].reshape(rows, D_MODEL).astype(_F32)
        y = yb if y is None else y + yb
    out = jnp.dot(y.astype(_BF16), wout_ref[...], preferred_element_type=_F32)
    mod = mod_ref[pl.program_id(0) if per_row_mod else 0]
    xn = x_ref[...].reshape(rows, D_MODEL) + mod[:, 2 * D_MODEL:] * out
    if final_norm:
        fnw_ref, o_ref = rest
        o_ref[...] = _rms(xn, fnw_ref[...]).reshape(nb, tile, D_MODEL)
    else:
        (o_ref,) = rest
        o_ref[...] = xn.reshape(nb, tile, D_MODEL)


def _merge(us, gz, weights, x, mod, fnw=None, *, tile, name, nb=1):
    B, N, D = x.shape
    assert B % nb == 0 and (nb == 1 or mod.shape[0] == 1)
    tok = lambda w: pl.BlockSpec((nb, tile, w), lambda b, j: (b, j, 0))
    in_specs = ([tok(BRANCH_W)] * 3 + [tok(G_W)] + [_const_spec(w.shape) for w in weights]
                + [tok(D), _const_spec(mod.shape)])
    args = [*us, gz, *weights, x, mod]
    if fnw is not None:
        in_specs.append(_const_spec(fnw.shape))
        args.append(fnw)
    return pl.pallas_call(
        functools.partial(_merge_kernel, final_norm=fnw is not None, per_row_mod=mod.shape[0] == B),
        grid=(B // nb, N // tile),
        in_specs=in_specs,
        out_specs=tok(D),
        out_shape=jax.ShapeDtypeStruct((B, N, D), _F32),
        compiler_params=_params(2),
        name=name,
    )(*args)


def _axial_tables(d, n_lat):
    h = d // 2
    t = jnp.arange(n_lat, dtype=jnp.int32)
    pos_row = (t // GRID_W).astype(_F32)
    pos_col = (t % GRID_W).astype(_F32)
    freqs = ROPE_THETA ** (-jnp.arange(0, h, 2, dtype=_F32) / h)
    ang_r = pos_row[:, None] * freqs[None, :]
    ang_c = pos_col[:, None] * freqs[None, :]
    cr, sr, cc, sc = jnp.cos(ang_r), jnp.sin(ang_r), jnp.cos(ang_c), jnp.sin(ang_c)
    z = jnp.zeros_like(sr)
    cos = jnp.concatenate([cr, cr, cc, cc], axis=-1)
    sinm = jnp.concatenate([-sr, z, -sc, z], axis=-1)
    sinp = jnp.concatenate([z, sr, z, sc], axis=-1)
    return cos, sinm, sinp


def _rope_tables(n_lat):
    set64 = [jnp.tile(t, (1, 2)) for t in _axial_tables(HEAD_DIM, n_lat)]
    pad = LANES - 2 * MLA_ROPE
    set32 = [jnp.concatenate([t, t, jnp.full((n_lat, pad), f, _F32)], axis=1)
             for t, f in zip(_axial_tables(MLA_ROPE, n_lat), (1.0, 0.0, 0.0))]
    return jnp.concatenate(set64 + set32, axis=1)


def _permute_heads(w, axis):
    shape = w.shape
    w = w.reshape(shape[:axis] + (KV_HEADS, HEADS // KV_HEADS, HEAD_DIM) + shape[axis + 1:])
    return jnp.swapaxes(w, axis, axis + 1).reshape(shape)


def _layer_weights(w_in, w_uq, w_ukv, w_o_swa, w_o_ax):
    D = w_in.shape[0]
    offs = {}
    o = 0
    for name, w in (("ckv", MLA_KV_RANK), ("kr", MLA_ROPE), ("sk", LANES), ("sv", LANES),
                    ("ak", LANES), ("av", LANES), ("cq", MLA_Q_RANK), ("sq", BRANCH_W),
                    ("aq", BRANCH_W), ("zm", BRANCH_W), ("zs", BRANCH_W), ("za", BRANCH_W),
                    ("g", G_W)):
        offs[name] = w_in[:, o:o + w]
        o += w
    kr = offs["kr"]
    kr2 = jnp.concatenate([kr, kr, jnp.zeros((D, LANES - 2 * MLA_ROPE), w_in.dtype)], axis=1)
    wkv = jnp.concatenate([offs["ckv"], kr2, offs["sk"], offs["sv"], offs["ak"], offs["av"]], axis=1)
    wq = jnp.concatenate([offs["cq"], _permute_heads(offs["sq"], 1), _permute_heads(offs["aq"], 1),
                          offs["zm"], _permute_heads(offs["zs"], 1), _permute_heads(offs["za"], 1),
                          offs["g"]], axis=1)
    uq = w_uq.reshape(MLA_Q_RANK, N_PAIRS, 2, HEAD_DIM + MLA_ROPE)
    nope = uq[..., :HEAD_DIM].reshape(MLA_Q_RANK, N_PAIRS, 2 * HEAD_DIM)
    rope = uq[..., HEAD_DIM:].reshape(MLA_Q_RANK, N_PAIRS, 2 * MLA_ROPE)
    pad = jnp.zeros((MLA_Q_RANK, N_PAIRS, MLA_QK_W - 2 * HEAD_DIM - 2 * MLA_ROPE), w_uq.dtype)
    wuq = jnp.concatenate([nope, rope, pad], axis=-1).reshape(MLA_Q_RANK, N_PAIRS * MLA_QK_W)
    ukv = w_ukv.reshape(MLA_KV_RANK, HEADS, 2, HEAD_DIM)
    wukv = jnp.concatenate([ukv[:, :, 0].reshape(MLA_KV_RANK, BRANCH_W),
                            ukv[:, :, 1].reshape(MLA_KV_RANK, BRANCH_W)], axis=1)
    bf = lambda a: a.astype(_BF16)
    return (bf(wkv), bf(wq), bf(wuq), bf(wukv),
            bf(_permute_heads(w_o_swa, 0)), bf(_permute_heads(w_o_ax, 0)))


_GZ_Z_BLK = G_W // BRANCH_W
_VKV_KV_BLK = BRANCH_W // LANES


def _attend_all(sink, pq, pc, px=None, *, tag, nb=1):
    q, gz = pq[2], pq[3]
    z = lambda branch: (gz, BRANCH_W, _GZ_Z_BLK + branch)
    kv = lambda p, i: None if p is None else (p[1], LANES, _VKV_KV_BLK + i)
    mla_k = None if px is None else (px[0], None, 0)
    mla_v = None if px is None else (px[1], BRANCH_W, 0)
    return (_attention((q, N_PAIRS * MLA_QK_W, 0), z(0), (pc[0], None, 0), (pc[1], BRANCH_W, 0), mla_k, mla_v,
                       mla=True, name="attn_mla" + tag, nb=nb),
            _attention((q, BRANCH_W, 2), z(1), kv(pc, 0), kv(pc, 1), kv(px, 0), kv(px, 1),
                       sink=sink, name="attn_window" + tag, nb=nb),
            _attention((q, BRANCH_W, 3), z(2), kv(pc, 2), kv(pc, 3), kv(px, 2), kv(px, 3),
                       name="attn_axial" + tag, nb=nb))


def kernel(x, c, ctx, c_ctx, ada_w, ada_b, norm_w, w_in, mla_q_norm, mla_w_uq, mla_kv_norm, mla_w_ukv,
           swa_sink, ax_q_norm, ax_k_norm, w_o_mla, w_o_swa, w_o_ax, w_out, final_norm_w):
    B, T, D = x.shape
    depth = w_in.shape[0]
    merge_tile = min(MERGE_TILE, T)
    ctx_nb = CTX_ROWS_PER_STEP if B % CTX_ROWS_PER_STEP == 0 else 1
    ctx_proj_nb = CTX_PROJ_ROWS_PER_STEP if B % CTX_PROJ_ROWS_PER_STEP == 0 else 1
    assert (D, ctx.shape[1]) == (D_MODEL, CTX_LEN) and T % TOK_TILE == 0 and T % merge_tile == 0 and T >= BAND

    mod_rows = -(-(B + 1) // 8) * 8
    cc = jnp.concatenate([c, c_ctx[None, :], jnp.zeros((mod_rows - B - 1, D), c.dtype)], axis=0)
    mod = _modulation(cc, ada_w, ada_b)
    tab = _rope_tables(T)
    fnw = final_norm_w.reshape(1, D)
    two_heads = lambda w: jnp.tile(w, 2).reshape(1, LANES)

    for l in range(depth):
        update_ctx = l < depth - 1
        wkv, wq, wuq, wukv, wos, woa = _layer_weights(w_in[l], mla_w_uq[l], mla_w_ukv[l], w_o_swa[l], w_o_ax[l])
        proj_w = (wkv, wq, wuq, wukv, mla_q_norm[l].reshape(1, -1), mla_kv_norm[l].reshape(1, -1),
                  two_heads(ax_q_norm[l]), two_heads(ax_k_norm[l]))
        merge_w = (w_o_mla[l].astype(_BF16), wos, woa, w_out[l].astype(_BF16))
        nw = norm_w[l].reshape(1, D)
        modx = mod[l, :B].reshape(B, 1, 3 * D)
        modc = mod[l, B:B + 1].reshape(1, 1, 3 * D)

        pc = _project(ctx, modc, nw, None, proj_w, tile=CTX_LEN, with_q=update_ctx, name="project_ctx", nb=ctx_proj_nb)
        px = _project(x, modx, nw, tab, proj_w, tile=TOK_TILE, with_q=True, name="project")
        u3 = _attend_all(swa_sink[l], px, pc, px, tag="")
        if update_ctx:
            uc3 = _attend_all(swa_sink[l], pc, pc, tag="_ctx", nb=ctx_nb)
            ctx = _merge(uc3, pc[3], merge_w, ctx, modc, tile=CTX_LEN, name="merge_ctx", nb=ctx_nb)
            x = _merge(u3, px[3], merge_w, x, modx, tile=merge_tile, name="merge")
        else:
            x = _merge(u3, px[3], merge_w, x, modx, fnw, tile=merge_tile, name="merge_final")
    return x
```

```python
import functools

import jax
import jax.numpy as jnp
from jax import lax
from jax.experimental import pallas as pl
from jax.experimental.pallas import tpu as pltpu

D_MODEL = 1024
CTX_LEN = 256
GRID_W = 64
ROPE_THETA = 10000.0
RMS_EPS = 1e-6
NEG_INF = -1e30
N_BRANCH = 3
WINDOW = 128

HEADS = 8
KV_HEADS = 2
HEAD_DIM = 64
MLA_ROPE = 32
MLA_Q_RANK = 384
MLA_KV_RANK = 256
LOG2E = 1.4426950408889634
MLA_SCALE = (HEAD_DIM + MLA_ROPE) ** -0.5 * LOG2E
GQA_SCALE = HEAD_DIM ** -0.5 * LOG2E
BRANCH_W = HEADS * HEAD_DIM

LANES = 128
N_PAIRS = BRANCH_W // LANES
MLA_QK_W = 2 * LANES

TOK_TILE = 512
MERGE_TILE = 1024
CTX_ROWS_PER_STEP = 4
CTX_PROJ_ROWS_PER_STEP = 2
Q_TILE = 256
BAND = Q_TILE + 2 * WINDOW
KEY_TILE = 256
ATTN_UNROLL = 4

KV_W = MLA_KV_RANK + 5 * LANES
Z_W = 3 * BRANCH_W
G_W = N_BRANCH * D_MODEL
TAB_W = 6 * LANES

VMEM_LIMIT = 56 * 1024 * 1024

_F32 = jnp.float32
_BF16 = jnp.bfloat16


def _params(n_axes):
    return pltpu.CompilerParams(
        dimension_semantics=("arbitrary",) * n_axes, vmem_limit_bytes=VMEM_LIMIT)


def _const_spec(shape):
    nd = len(shape)
    return pl.BlockSpec(shape, lambda *_: (0,) * nd, pipeline_mode=pl.Buffered(1))


def _mod_kernel(c_ref, w_ref, b_ref, o_ref):
    c = c_ref[...]
    s = c / (1.0 + jnp.exp(-c))
    o_ref[0] = jnp.dot(s, w_ref[0], preferred_element_type=_F32,
                       precision=lax.Precision.HIGHEST) + b_ref[0]


def _modulation(cc, ada_w, ada_b):
    depth = ada_w.shape[0]
    rows = cc.shape[0]
    col_tile = D_MODEL
    return pl.pallas_call(
        _mod_kernel,
        grid=(depth, (3 * D_MODEL) // col_tile),
        in_specs=[
            pl.BlockSpec((rows, D_MODEL), lambda l, n: (0, 0)),
            pl.BlockSpec((1, D_MODEL, col_tile), lambda l, n: (l, 0, n)),
            pl.BlockSpec((1, 1, col_tile), lambda l, n: (l, 0, n)),
        ],
        out_specs=pl.BlockSpec((1, rows, col_tile), lambda l, n: (l, 0, n)),
        out_shape=jax.ShapeDtypeStruct((depth, rows, 3 * D_MODEL), _F32),
        compiler_params=_params(2),
        name="modulation",
    )(cc, ada_w, ada_b.reshape(depth, 1, 3 * D_MODEL))


def _rms(x, w):
    ms = jnp.mean(x * x, axis=-1, keepdims=True)
    return x * lax.rsqrt(ms + RMS_EPS) * w


def _head_rms(x, w):
    lane = lax.broadcasted_iota(jnp.int32, x.shape, 1)
    y = x * x
    s = 1
    while s < HEAD_DIM:
        up = pltpu.roll(y, s, 1)
        dn = pltpu.roll(y, LANES - s, 1)
        y = y + jnp.where((lane & s) != 0, up, dn)
        s *= 2
    return x * lax.rsqrt(y * (1.0 / HEAD_DIM) + RMS_EPS) * w


def _rope(x, tab_ref, wide):
    if tab_ref is None:
        return x
    k, r = (0, HEAD_DIM // 4) if wide else (1, MLA_ROPE // 4)
    cos = tab_ref[:, (3 * k) * LANES:(3 * k + 1) * LANES]
    sinm = tab_ref[:, (3 * k + 1) * LANES:(3 * k + 2) * LANES]
    sinp = tab_ref[:, (3 * k + 2) * LANES:(3 * k + 3) * LANES]
    return x * cos + pltpu.roll(x, LANES - r, 1) * sinm + pltpu.roll(x, r, 1) * sinp


def _proj_kernel(*refs, has_rope, with_q, per_row_mod):
    refs = list(refs)
    x_ref, mod_ref, nw_ref = refs[:3]
    del refs[:3]
    tab_ref = refs.pop(0) if has_rope else None
    wkv_ref, wq_ref, wuq_ref, wukv_ref, qn_ref, kvn_ref, axq_ref, axk_ref = refs[:8]
    km_ref, vkv_ref = refs[8:10]

    nb, tile, _ = x_ref.shape

    def put(ref, cols, val):
        ref[:, :, cols] = val.reshape(nb, tile, val.shape[-1])

    mod = mod_ref[pl.program_id(1) if per_row_mod else 0]
    x = x_ref[...].reshape(nb * tile, D_MODEL)
    gain = nw_ref[...] * (1.0 + mod[:, D_MODEL:2 * D_MODEL])
    inv = lax.rsqrt(jnp.mean(x * x, axis=-1, keepdims=True) + RMS_EPS)
    h = (x * inv * gain + mod[:, :D_MODEL]).astype(_BF16)

    pkv = jnp.dot(h, wkv_ref[...], preferred_element_type=_F32)
    ckvn = _rms(pkv[:, :MLA_KV_RANK], kvn_ref[...]).astype(_BF16)
    kvm = jnp.dot(ckvn, wukv_ref[...], preferred_element_type=_F32)
    put(vkv_ref, slice(0, BRANCH_W), kvm[:, BRANCH_W:].astype(_BF16))
    o = MLA_KV_RANK
    kr2 = _rope(pkv[:, o:o + LANES], tab_ref, False).astype(_BF16)
    for p in range(N_PAIRS):
        kn = kvm[:, p * LANES:(p + 1) * LANES]
        lo, hi = p * MLA_QK_W, p * MLA_QK_W + LANES
        if has_rope:
            km_ref[0, lo:hi, :] = kn.T.astype(_BF16)
            km_ref[0, hi:hi + LANES, :] = kr2.T
        else:
            put(km_ref, slice(lo, hi), kn.astype(_BF16))
            put(km_ref, slice(hi, hi + LANES), kr2)
    o += LANES
    put(vkv_ref, slice(BRANCH_W, BRANCH_W + LANES), _rope(pkv[:, o:o + LANES], tab_ref, True).astype(_BF16))
    o += LANES
    put(vkv_ref, slice(BRANCH_W + LANES, BRANCH_W + 2 * LANES), pkv[:, o:o + LANES].astype(_BF16))
    o += LANES
    put(vkv_ref, slice(BRANCH_W + 2 * LANES, BRANCH_W + 3 * LANES),
        _rope(_head_rms(pkv[:, o:o + LANES], axk_ref[...]), tab_ref, True).astype(_BF16))
    o += LANES
    put(vkv_ref, slice(BRANCH_W + 3 * LANES, BRANCH_W + 4 * LANES), pkv[:, o:o + LANES].astype(_BF16))
    if not with_q:
        return

    q_ref, gz_ref = refs[10:]
    cq = jnp.dot(h, wq_ref[:, :MLA_Q_RANK], preferred_element_type=_F32)
    cqn = _rms(cq, qn_ref[...]).astype(_BF16)
    qm = jnp.dot(cqn, wuq_ref[...], preferred_element_type=_F32)
    for p in range(N_PAIRS):
        lo = p * MLA_QK_W
        put(q_ref, slice(lo, lo + LANES), (qm[:, lo:lo + LANES] * MLA_SCALE).astype(_BF16))
        put(q_ref, slice(lo + LANES, lo + MLA_QK_W),
            (_rope(qm[:, lo + LANES:lo + MLA_QK_W], tab_ref, False) * MLA_SCALE).astype(_BF16))
    c0 = MLA_Q_RANK
    sq = jnp.dot(h, wq_ref[:, c0:c0 + BRANCH_W], preferred_element_type=_F32)
    c0 += BRANCH_W
    aq = jnp.dot(h, wq_ref[:, c0:c0 + BRANCH_W], preferred_element_type=_F32)
    c0 += BRANCH_W
    for p in range(N_PAIRS):
        blk = slice(p * LANES, (p + 1) * LANES)
        qs0 = N_PAIRS * MLA_QK_W + p * LANES
        put(q_ref, slice(qs0, qs0 + LANES), (_rope(sq[:, blk], tab_ref, True) * GQA_SCALE).astype(_BF16))
        aqn = _head_rms(aq[:, blk], axq_ref[...])
        put(q_ref, slice(qs0 + BRANCH_W, qs0 + BRANCH_W + LANES), (_rope(aqn, tab_ref, True) * GQA_SCALE).astype(_BF16))
    for n in range(Z_W // BRANCH_W):
        z = jnp.dot(h, wq_ref[:, c0:c0 + BRANCH_W], preferred_element_type=_F32)
        put(gz_ref, slice(G_W + n * BRANCH_W, G_W + (n + 1) * BRANCH_W), (z / (1.0 + jnp.exp(-z))).astype(_BF16))
        c0 += BRANCH_W
    for n in range(G_W // BRANCH_W):
        g = jnp.dot(h, wq_ref[:, c0:c0 + BRANCH_W], preferred_element_type=_F32)
        put(gz_ref, slice(n * BRANCH_W, (n + 1) * BRANCH_W), (1.0 / (1.0 + jnp.exp(-g))).astype(_BF16))
        c0 += BRANCH_W


def _project(x, mod, norm_w, tab, weights, *, tile, with_q, name, nb=1):
    B, N, D = x.shape
    assert B % nb == 0 and (nb == 1 or (mod.shape[0] == 1 and tab is None))
    widths = [N_PAIRS * MLA_QK_W, BRANCH_W + 4 * LANES]
    if with_q:
        widths += [N_PAIRS * MLA_QK_W + 2 * BRANCH_W, G_W + Z_W]
    in_specs = [pl.BlockSpec((nb, tile, D), lambda j, b: (b, j, 0)),
                _const_spec(mod.shape),
                _const_spec(norm_w.shape)]
    args = [x, mod, norm_w]
    if tab is not None:
        in_specs.append(pl.BlockSpec((tile, TAB_W), lambda j, b: (j, 0)))
        args.append(tab)
    in_specs += [_const_spec(w.shape) for w in weights]
    out_specs = [pl.BlockSpec((nb, tile, w), lambda j, b: (b, j, 0)) for w in widths]
    out_shape = [jax.ShapeDtypeStruct((B, N, w), _BF16) for w in widths]
    if tab is not None:
        out_specs[0] = pl.BlockSpec((1, widths[0], tile), lambda j, b: (b, 0, j))
        out_shape[0] = jax.ShapeDtypeStruct((B, widths[0], N), _BF16)
    return pl.pallas_call(
        functools.partial(_proj_kernel, has_rope=tab is not None, with_q=with_q, per_row_mod=mod.shape[0] == B),
        grid=(N // tile, B // nb),
        in_specs=in_specs,
        out_specs=out_specs,
        out_shape=out_shape,
        compiler_params=_params(2),
        name=name,
    )(*args, *weights)


def _scores(q, k):
    return lax.dot_general(q, k, (((1,), (1,)), ((), ())), preferred_element_type=_F32)


def _stack_heads(q, mla):
    lane = lax.broadcasted_iota(jnp.int32, q.shape, 1)
    is_a = lane < HEAD_DIM
    if mla:
        is_a = is_a | ((lane >= LANES) & (lane < LANES + MLA_ROPE))
    zero = jnp.zeros_like(q)
    return jnp.concatenate([jnp.where(is_a, q, zero), jnp.where(is_a, zero, q)], axis=0)


def _fill_v1(v1_scr, v_ref, bi, n_blocks):
    n = v_ref.shape[1]
    for j in range(n_blocks):
        v1_scr[j, :, :LANES] = v_ref[bi, :, j * LANES:(j + 1) * LANES]
        v1_scr[j, :, LANES:] = jnp.ones((n, LANES), _BF16)


def _online_softmax_pv(q2, key_tiles):
    m = acc = None
    for k, v1, _, k_is_transposed in key_tiles:
        s = jnp.dot(q2, k, preferred_element_type=_F32) if k_is_transposed else _scores(q2, k)
        m_new = jnp.max(s, axis=-1, keepdims=True)
        if m is not None:
            m_new = jnp.maximum(m, m_new)
        pv = jnp.dot(jnp.exp2(s - m_new).astype(_BF16), v1, preferred_element_type=_F32)
        acc = pv if acc is None else jnp.exp2(m - m_new) * acc + pv
        m = m_new
    return acc[:, :LANES] / acc[:, LANES:]


def _sink_softmax_pv(q2, key_tiles, sink):
    scores = []
    m = sink
    for k, _, bias, _ in key_tiles:
        s = _scores(q2, k)
        if bias is not None:
            s = s + bias
        scores.append(s)
        m = jnp.maximum(m, jnp.max(s, axis=-1, keepdims=True))
    acc = None
    for s, (_, v1, _, _) in zip(scores, key_tiles):
        pv = jnp.dot(jnp.exp2(s - m).astype(_BF16), v1, preferred_element_type=_F32)
        acc = pv if acc is None else acc + pv
    return acc[:, :LANES] / (acc[:, LANES:] + jnp.exp2(sink - m))


def _gate_store(o_ref, z_ref, bi, rows, p, r):
    n = r.shape[0] // 2
    lane = lax.broadcasted_iota(jnp.int32, (n, LANES), 1)
    o = jnp.where(lane < HEAD_DIM, r[:n], r[n:])
    blk = slice(p * LANES, (p + 1) * LANES)
    o_ref[bi, rows, blk] = (o * z_ref[bi, rows, blk].astype(_F32)).astype(_BF16)


def _tile_rows(tile):
    return pl.ds(pl.multiple_of(tile * Q_TILE, Q_TILE), Q_TILE)


def _attn_kernel(*refs, mla, window, latent_keys):
    refs = list(refs)
    sink_ref = refs.pop(0) if window else None
    q_ref, z_ref, kc_ref, vc_ref = refs[:4]
    k_ref, v_ref = refs[4:6] if latent_keys else (None, None)
    o_ref, vc1_scr = refs[-2 - latent_keys], refs[-1 - latent_keys]
    v1_scr = refs[-1] if latent_keys else None
    for bi in range(q_ref.shape[0]):
        _attn_row(bi, sink_ref, q_ref, z_ref, kc_ref, vc_ref, k_ref, v_ref, o_ref, vc1_scr, v1_scr,
                  mla=mla, window=window)


def _attn_row(bi, sink_ref, q_ref, z_ref, kc_ref, vc_ref, k_ref, v_ref, o_ref, vc1_scr, v1_scr, *, mla, window):
    latent_keys = k_ref is not None
    n_tiles = q_ref.shape[1] // Q_TILE
    wq = MLA_QK_W if mla else LANES
    n_blocks = N_PAIRS if mla else 1
    vc1_scr = vc1_scr.at[bi]
    _fill_v1(vc1_scr, vc_ref, bi, n_blocks)
    if latent_keys:
        _fill_v1(v1_scr, v_ref, bi, n_blocks)
        n_lat = v_ref.shape[1]

    def ctx_tile(p):
        kb = slice(p * wq, (p + 1) * wq) if mla else slice(None)
        return (kc_ref[bi, :, kb], vc1_scr[p if mla else 0], None, False)

    def latent_tile(p, start, bias):
        keys = pl.ds(start, KEY_TILE)
        if mla:
            return (k_ref[bi, p * wq:(p + 1) * wq, keys], v1_scr[p, keys, :], bias, True)
        return (k_ref[bi, keys, :], v1_scr[0, keys, :], bias, False)

    def pair_sink(p):
        row = lax.broadcasted_iota(jnp.int32, (2 * Q_TILE, 1), 0)
        return jnp.where(row < Q_TILE, sink_ref[p], sink_ref[N_PAIRS + p]) * LOG2E

    def step(t, carry):
        if window and latent_keys:
            q0 = t * Q_TILE
            k0 = jnp.clip(q0 - WINDOW, 0, n_lat - BAND)
            shape = (2 * Q_TILE, KEY_TILE)
            col = lax.broadcasted_iota(jnp.int32, shape, 1)
            row = lax.broadcasted_iota(jnp.int32, shape, 0) & (Q_TILE - 1)
            lat = [(pl.multiple_of(k0 + b0, LANES),
                    jnp.where(jnp.abs((col + (k0 + b0)) - (row + q0)) <= WINDOW, 0.0, NEG_INF))
                   for b0 in range(0, BAND, KEY_TILE)]
        elif latent_keys:
            lat = [(k0, None) for k0 in range(0, n_lat, KEY_TILE)]
        else:
            lat = []
        rows = _tile_rows(t)
        for p in range(N_PAIRS):
            tiles = [latent_tile(p, start, bias) for start, bias in lat] + [ctx_tile(p)]
            q2 = _stack_heads(q_ref[bi, rows, p * wq:(p + 1) * wq], mla)
            r = _sink_softmax_pv(q2, tiles, pair_sink(p)) if window else _online_softmax_pv(q2, tiles)
            _gate_store(o_ref, z_ref, bi, rows, p, r)
        return carry

    lax.fori_loop(0, n_tiles, step, 0, unroll=min(n_tiles, ATTN_UNROLL))


def _attention(q, z, kc, vc, k=None, v=None, *, mla=False, sink=None, name, nb=1):
    B, nq, _ = q[0].shape
    latent_keys = k is not None

    def view(operand):
        a, width, blk = operand
        if width is None:
            return pl.BlockSpec((nb,) + a.shape[1:], lambda b: (b, 0, 0))
        return pl.BlockSpec((nb, a.shape[1], width), lambda b: (b, 0, blk))

    operands = [q, z, kc, vc] + ([k, v] if latent_keys else [])
    args = [o[0] for o in operands]
    in_specs = [view(o) for o in operands]
    if sink is not None:
        args.insert(0, sink)
        in_specs.insert(0, pl.BlockSpec(memory_space=pltpu.SMEM))
    n_blocks = N_PAIRS if mla else 1
    scratch = [pltpu.VMEM((nb, n_blocks, CTX_LEN, 2 * LANES), _BF16)]
    if latent_keys:
        scratch.append(pltpu.VMEM((n_blocks, v[0].shape[1], 2 * LANES), _BF16))
    return pl.pallas_call(
        functools.partial(_attn_kernel, mla=mla, window=sink is not None, latent_keys=latent_keys),
        grid=(B // nb,),
        in_specs=in_specs,
        out_specs=pl.BlockSpec((nb, nq, BRANCH_W), lambda b: (b, 0, 0)),
        out_shape=jax.ShapeDtypeStruct((B, nq, BRANCH_W), _BF16),
        scratch_shapes=scratch,
        compiler_params=_params(1),
        name=name,
    )(*args)


def _merge_kernel(um_ref, us_ref, ua_ref, sg_ref, wom_ref, wos_ref, woa_ref, wout_ref,
                  x_ref, mod_ref, *rest, final_norm, per_row_mod):
    nb, tile, _ = x_ref.shape
    rows = nb * tile
    y = None
    for n, (u_ref, w_ref) in enumerate(((um_ref, wom_ref), (us_ref, wos_ref), (ua_ref, woa_ref))):
        yb = jnp.dot(u_ref[...].reshape(rows, BRANCH_W), w_ref[...], preferred_element_type=_F32)
        yb = yb * sg_ref[:, :, n * D_MODEL:(n + 1) * D_MODEL].reshape(rows, D_MODEL).astype(_F32)
        y = yb if y is None else y + yb
    out = jnp.dot(y.astype(_BF16), wout_ref[...], preferred_element_type=_F32)
    mod = mod_ref[pl.program_id(0) if per_row_mod else 0]
    xn = x_ref[...].reshape(rows, D_MODEL) + mod[:, 2 * D_MODEL:] * out
    if final_norm:
        fnw_ref, o_ref = rest
        o_ref[...] = _rms(xn, fnw_ref[...]).reshape(nb, tile, D_MODEL)
    else:
        (o_ref,) = rest
        o_ref[...] = xn.reshape(nb, tile, D_MODEL)


def _merge(us, gz, weights, x, mod, fnw=None, *, tile, name, nb=1):
    B, N, D = x.shape
    assert B % nb == 0 and (nb == 1 or mod.shape[0] == 1)
    tok = lambda w: pl.BlockSpec((nb, tile, w), lambda b, j: (b, j, 0))
    in_specs = ([tok(BRANCH_W)] * 3 + [tok(G_W)] + [_const_spec(w.shape) for w in weights]
                + [tok(D), _const_spec(mod.shape)])
    args = [*us, gz, *weights, x, mod]
    if fnw is not None:
        in_specs.append(_const_spec(fnw.shape))
        args.append(fnw)
    return pl.pallas_call(
        functools.partial(_merge_kernel, final_norm=fnw is not None, per_row_mod=mod.shape[0] == B),
        grid=(B // nb, N // tile),
        in_specs=in_specs,
        out_specs=tok(D),
        out_shape=jax.ShapeDtypeStruct((B, N, D), _F32),
        compiler_params=_params(2),
        name=name,
    )(*args)


def _axial_tables(d, n_lat):
    h = d // 2
    t = jnp.arange(n_lat, dtype=jnp.int32)
    pos_row = (t // GRID_W).astype(_F32)
    pos_col = (t % GRID_W).astype(_F32)
    freqs = ROPE_THETA ** (-jnp.arange(0, h, 2, dtype=_F32) / h)
    ang_r = pos_row[:, None] * freqs[None, :]
    ang_c = pos_col[:, None] * freqs[None, :]
    cr, sr, cc, sc = jnp.cos(ang_r), jnp.sin(ang_r), jnp.cos(ang_c), jnp.sin(ang_c)
    z = jnp.zeros_like(sr)
    cos = jnp.concatenate([cr, cr, cc, cc], axis=-1)
    sinm = jnp.concatenate([-sr, z, -sc, z], axis=-1)
    sinp = jnp.concatenate([z, sr, z, sc], axis=-1)
    return cos, sinm, sinp


def _rope_tables(n_lat):
    set64 = [jnp.tile(t, (1, 2)) for t in _axial_tables(HEAD_DIM, n_lat)]
    pad = LANES - 2 * MLA_ROPE
    set32 = [jnp.concatenate([t, t, jnp.full((n_lat, pad), f, _F32)], axis=1)
             for t, f in zip(_axial_tables(MLA_ROPE, n_lat), (1.0, 0.0, 0.0))]
    return jnp.concatenate(set64 + set32, axis=1)


def _permute_heads(w, axis):
    shape = w.shape
    w = w.reshape(shape[:axis] + (KV_HEADS, HEADS // KV_HEADS, HEAD_DIM) + shape[axis + 1:])
    return jnp.swapaxes(w, axis, axis + 1).reshape(shape)


def _layer_weights(w_in, w_uq, w_ukv, w_o_swa, w_o_ax):
    D = w_in.shape[0]
    offs = {}
    o = 0
    for name, w in (("ckv", MLA_KV_RANK), ("kr", MLA_ROPE), ("sk", LANES), ("sv", LANES),
                    ("ak", LANES), ("av", LANES), ("cq", MLA_Q_RANK), ("sq", BRANCH_W),
                    ("aq", BRANCH_W), ("zm", BRANCH_W), ("zs", BRANCH_W), ("za", BRANCH_W),
                    ("g", G_W)):
        offs[name] = w_in[:, o:o + w]
        o += w
    kr = offs["kr"]
    kr2 = jnp.concatenate([kr, kr, jnp.zeros((D, LANES - 2 * MLA_ROPE), w_in.dtype)], axis=1)
    wkv = jnp.concatenate([offs["ckv"], kr2, offs["sk"], offs["sv"], offs["ak"], offs["av"]], axis=1)
    wq = jnp.concatenate([offs["cq"], _permute_heads(offs["sq"], 1), _permute_heads(offs["aq"], 1),
                          offs["zm"], _permute_heads(offs["zs"], 1), _permute_heads(offs["za"], 1),
                          offs["g"]], axis=1)
    uq = w_uq.reshape(MLA_Q_RANK, N_PAIRS, 2, HEAD_DIM + MLA_ROPE)
    nope = uq[..., :HEAD_DIM].reshape(MLA_Q_RANK, N_PAIRS, 2 * HEAD_DIM)
    rope = uq[..., HEAD_DIM:].reshape(MLA_Q_RANK, N_PAIRS, 2 * MLA_ROPE)
    pad = jnp.zeros((MLA_Q_RANK, N_PAIRS, MLA_QK_W - 2 * HEAD_DIM - 2 * MLA_ROPE), w_uq.dtype)
    wuq = jnp.concatenate([nope, rope, pad], axis=-1).reshape(MLA_Q_RANK, N_PAIRS * MLA_QK_W)
    ukv = w_ukv.reshape(MLA_KV_RANK, HEADS, 2, HEAD_DIM)
    wukv = jnp.concatenate([ukv[:, :, 0].reshape(MLA_KV_RANK, BRANCH_W),
                            ukv[:, :, 1].reshape(MLA_KV_RANK, BRANCH_W)], axis=1)
    bf = lambda a: a.astype(_BF16)
    return (bf(wkv), bf(wq), bf(wuq), bf(wukv),
            bf(_permute_heads(w_o_swa, 0)), bf(_permute_heads(w_o_ax, 0)))


_GZ_Z_BLK = G_W // BRANCH_W
_VKV_KV_BLK = BRANCH_W // LANES


def _attend_all(sink, pq, pc, px=None, *, tag, nb=1):
    q, gz = pq[2], pq[3]
    z = lambda branch: (gz, BRANCH_W, _GZ_Z_BLK + branch)
    kv = lambda p, i: None if p is None else (p[1], LANES, _VKV_KV_BLK + i)
    mla_k = None if px is None else (px[0], None, 0)
    mla_v = None if px is None else (px[1], BRANCH_W, 0)
    return (_attention((q, N_PAIRS * MLA_QK_W, 0), z(0), (pc[0], None, 0), (pc[1], BRANCH_W, 0), mla_k, mla_v,
                       mla=True, name="attn_mla" + tag, nb=nb),
            _attention((q, BRANCH_W, 2), z(1), kv(pc, 0), kv(pc, 1), kv(px, 0), kv(px, 1),
                       sink=sink, name="attn_window" + tag, nb=nb),
            _attention((q, BRANCH_W, 3), z(2), kv(pc, 2), kv(pc, 3), kv(px, 2), kv(px, 3),
                       name="attn_axial" + tag, nb=nb))


def kernel(x, c, ctx, c_ctx, ada_w, ada_b, norm_w, w_in, mla_q_norm, mla_w_uq, mla_kv_norm, mla_w_ukv,
           swa_sink, ax_q_norm, ax_k_norm, w_o_mla, w_o_swa, w_o_ax, w_out, final_norm_w):
    B, T, D = x.shape
    depth = w_in.shape[0]
    merge_tile = min(MERGE_TILE, T)
    ctx_nb = CTX_ROWS_PER_STEP if B % CTX_ROWS_PER_STEP == 0 else 1
    ctx_proj_nb = CTX_PROJ_ROWS_PER_STEP if B % CTX_PROJ_ROWS_PER_STEP == 0 else 1
    assert (D, ctx.shape[1]) == (D_MODEL, CTX_LEN) and T % TOK_TILE == 0 and T % merge_tile == 0 and T >= BAND

    mod_rows = -(-(B + 1) // 8) * 8
    cc = jnp.concatenate([c, c_ctx[None, :], jnp.zeros((mod_rows - B - 1, D), c.dtype)], axis=0)
    mod = _modulation(cc, ada_w, ada_b)
    tab = _rope_tables(T)
    fnw = final_norm_w.reshape(1, D)
    two_heads = lambda w: jnp.tile(w, 2).reshape(1, LANES)

    for l in range(depth):
        update_ctx = l < depth - 1
        wkv, wq, wuq, wukv, wos, woa = _layer_weights(w_in[l], mla_w_uq[l], mla_w_ukv[l], w_o_swa[l], w_o_ax[l])
        proj_w = (wkv, wq, wuq, wukv, mla_q_norm[l].reshape(1, -1), mla_kv_norm[l].reshape(1, -1),
                  two_heads(ax_q_norm[l]), two_heads(ax_k_norm[l]))
        merge_w = (w_o_mla[l].astype(_BF16), wos, woa, w_out[l].astype(_BF16))
        nw = norm_w[l].reshape(1, D)
        modx = mod[l, :B].reshape(B, 1, 3 * D)
        modc = mod[l, B:B + 1].reshape(1, 1, 3 * D)

        pc = _project(ctx, modc, nw, None, proj_w, tile=CTX_LEN, with_q=update_ctx, name="project_ctx", nb=ctx_proj_nb)
        px = _project(x, modx, nw, tab, proj_w, tile=TOK_TILE, with_q=True, name="project")
        u3 = _attend_all(swa_sink[l], px, pc, px, tag="")
        if update_ctx:
            uc3 = _attend_all(swa_sink[l], pc, pc, tag="_ctx", nb=ctx_nb)
            ctx = _merge(uc3, pc[3], merge_w, ctx, modc, tile=CTX_LEN, name="merge_ctx", nb=ctx_nb)
            x = _merge(u3, px[3], merge_w, x, modx, tile=merge_tile, name="merge")
        else:
            x = _merge(u3, px[3], merge_w, x, modx, fnw, tile=merge_tile, name="merge_final")
    return x
```

```python
import functools

import jax
import jax.numpy as jnp
from jax import lax
from jax.experimental import pallas as pl
from jax.experimental.pallas import tpu as pltpu

D_MODEL = 1024
CTX_LEN = 256
GRID_W = 64
ROPE_THETA = 10000.0
RMS_EPS = 1e-6
NEG_INF = -1e30
N_BRANCH = 3
WINDOW = 128

HEADS = 8
KV_HEADS = 2
HEAD_DIM = 64
MLA_ROPE = 32
MLA_Q_RANK = 384
MLA_KV_RANK = 256
LOG2E = 1.4426950408889634
MLA_SCALE = (HEAD_DIM + MLA_ROPE) ** -0.5 * LOG2E
GQA_SCALE = HEAD_DIM ** -0.5 * LOG2E
BRANCH_W = HEADS * HEAD_DIM

LANES = 128
N_PAIRS = BRANCH_W // LANES
MLA_QK_W = 2 * LANES

TOK_TILE = 512
MERGE_TILE = 1024
CTX_ROWS_PER_STEP = 4
CTX_PROJ_ROWS_PER_STEP = 2
Q_TILE = 256
BAND = Q_TILE + 2 * WINDOW
KEY_TILE = 256
ATTN_UNROLL = 4

Z_W = N_BRANCH * BRANCH_W
G_W = N_BRANCH * D_MODEL
TAB_W = 6 * LANES

VMEM_LIMIT = 56 * 1024 * 1024

_F32 = jnp.float32
_BF16 = jnp.bfloat16


def _params(n_axes):
    return pltpu.CompilerParams(
        dimension_semantics=("arbitrary",) * n_axes, vmem_limit_bytes=VMEM_LIMIT)


def _const_spec(shape):
    nd = len(shape)
    return pl.BlockSpec(shape, lambda *_: (0,) * nd, pipeline_mode=pl.Buffered(1))


def _mod_kernel(c_ref, w_ref, b_ref, o_ref):
    c = c_ref[...]
    s = c / (1.0 + jnp.exp(-c))
    o_ref[0] = jnp.dot(s, w_ref[0], preferred_element_type=_F32,
                       precision=lax.Precision.HIGHEST) + b_ref[0]


def _modulation(cc, ada_w, ada_b):
    depth = ada_w.shape[0]
    rows = cc.shape[0]
    col_tile = D_MODEL
    return pl.pallas_call(
        _mod_kernel,
        grid=(depth, (3 * D_MODEL) // col_tile),
        in_specs=[
            pl.BlockSpec((rows, D_MODEL), lambda l, n: (0, 0)),
            pl.BlockSpec((1, D_MODEL, col_tile), lambda l, n: (l, 0, n)),
            pl.BlockSpec((1, 1, col_tile), lambda l, n: (l, 0, n)),
        ],
        out_specs=pl.BlockSpec((1, rows, col_tile), lambda l, n: (l, 0, n)),
        out_shape=jax.ShapeDtypeStruct((depth, rows, 3 * D_MODEL), _F32),
        compiler_params=_params(2),
        name="modulation",
    )(cc, ada_w, ada_b.reshape(depth, 1, 3 * D_MODEL))


def _rms(x, w):
    ms = jnp.mean(x * x, axis=-1, keepdims=True)
    return x * lax.rsqrt(ms + RMS_EPS) * w


def _head_rms(x, w):
    lane = lax.broadcasted_iota(jnp.int32, x.shape, 1)
    y = x * x
    s = 1
    while s < HEAD_DIM:
        up = pltpu.roll(y, s, 1)
        dn = pltpu.roll(y, LANES - s, 1)
        y = y + jnp.where((lane & s) != 0, up, dn)
        s *= 2
    return x * lax.rsqrt(y * (1.0 / HEAD_DIM) + RMS_EPS) * w


def _rope(x, tab_ref, wide):
    if tab_ref is None:
        return x
    k, r = (0, HEAD_DIM // 4) if wide else (1, MLA_ROPE // 4)
    cos = tab_ref[:, (3 * k) * LANES:(3 * k + 1) * LANES]
    sinm = tab_ref[:, (3 * k + 1) * LANES:(3 * k + 2) * LANES]
    sinp = tab_ref[:, (3 * k + 2) * LANES:(3 * k + 3) * LANES]
    return x * cos + pltpu.roll(x, LANES - r, 1) * sinm + pltpu.roll(x, r, 1) * sinp


def _proj_kernel(*refs, has_rope, with_q, per_row_mod):
    refs = list(refs)
    x_ref, mod_ref, nw_ref = refs[:3]
    del refs[:3]
    tab_ref = refs.pop(0) if has_rope else None
    wkv_ref, wq_ref, wuq_ref, wukv_ref, qn_ref, kvn_ref, axq_ref, axk_ref = refs[:8]
    km_ref, vkv_ref = refs[8:10]

    nb, tile, _ = x_ref.shape

    def put(ref, cols, val):
        ref[:, :, cols] = val.reshape(nb, tile, val.shape[-1])

    mod = mod_ref[pl.program_id(1) if per_row_mod else 0]
    x = x_ref[...].reshape(nb * tile, D_MODEL)
    gain = nw_ref[...] * (1.0 + mod[:, D_MODEL:2 * D_MODEL])
    inv = lax.rsqrt(jnp.mean(x * x, axis=-1, keepdims=True) + RMS_EPS)
    h = (x * inv * gain + mod[:, :D_MODEL]).astype(_BF16)

    pkv = jnp.dot(h, wkv_ref[...], preferred_element_type=_F32)
    ckvn = _rms(pkv[:, :MLA_KV_RANK], kvn_ref[...]).astype(_BF16)
    kvm = jnp.dot(ckvn, wukv_ref[...], preferred_element_type=_F32)
    put(vkv_ref, slice(0, BRANCH_W), kvm[:, BRANCH_W:].astype(_BF16))
    o = MLA_KV_RANK
    kr2 = _rope(pkv[:, o:o + LANES], tab_ref, False).astype(_BF16)
    for p in range(N_PAIRS):
        kn = kvm[:, p * LANES:(p + 1) * LANES]
        lo, hi = p * MLA_QK_W, p * MLA_QK_W + LANES
        if has_rope:
            km_ref[0, lo:hi, :] = kn.T.astype(_BF16)
            km_ref[0, hi:hi + LANES, :] = kr2.T
        else:
            put(km_ref, slice(lo, hi), kn.astype(_BF16))
            put(km_ref, slice(hi, hi + LANES), kr2)
    o += LANES
    put(vkv_ref, slice(BRANCH_W, BRANCH_W + LANES), _rope(pkv[:, o:o + LANES], tab_ref, True).astype(_BF16))
    o += LANES
    put(vkv_ref, slice(BRANCH_W + LANES, BRANCH_W + 2 * LANES), pkv[:, o:o + LANES].astype(_BF16))
    o += LANES
    put(vkv_ref, slice(BRANCH_W + 2 * LANES, BRANCH_W + 3 * LANES),
        _rope(_head_rms(pkv[:, o:o + LANES], axk_ref[...]), tab_ref, True).astype(_BF16))
    o += LANES
    put(vkv_ref, slice(BRANCH_W + 3 * LANES, BRANCH_W + 4 * LANES), pkv[:, o:o + LANES].astype(_BF16))
    if not with_q:
        return

    q_ref, gz_ref = refs[10:]
    cq = jnp.dot(h, wq_ref[:, :MLA_Q_RANK], preferred_element_type=_F32)
    cqn = _rms(cq, qn_ref[...]).astype(_BF16)
    qm = jnp.dot(cqn, wuq_ref[...], preferred_element_type=_F32)
    for p in range(N_PAIRS):
        lo = p * MLA_QK_W
        put(q_ref, slice(lo, lo + LANES), (qm[:, lo:lo + LANES] * MLA_SCALE).astype(_BF16))
        put(q_ref, slice(lo + LANES, lo + MLA_QK_W),
            (_rope(qm[:, lo + LANES:lo + MLA_QK_W], tab_ref, False) * MLA_SCALE).astype(_BF16))
    c0 = MLA_Q_RANK
    sq = jnp.dot(h, wq_ref[:, c0:c0 + BRANCH_W], preferred_element_type=_F32)
    c0 += BRANCH_W
    aq = jnp.dot(h, wq_ref[:, c0:c0 + BRANCH_W], preferred_element_type=_F32)
    c0 += BRANCH_W
    for p in range(N_PAIRS):
        blk = slice(p * LANES, (p + 1) * LANES)
        qs0 = N_PAIRS * MLA_QK_W + p * LANES
        put(q_ref, slice(qs0, qs0 + LANES), (_rope(sq[:, blk], tab_ref, True) * GQA_SCALE).astype(_BF16))
        aqn = _head_rms(aq[:, blk], axq_ref[...])
        put(q_ref, slice(qs0 + BRANCH_W, qs0 + BRANCH_W + LANES), (_rope(aqn, tab_ref, True) * GQA_SCALE).astype(_BF16))
    for n in range(Z_W // BRANCH_W):
        z = jnp.dot(h, wq_ref[:, c0:c0 + BRANCH_W], preferred_element_type=_F32)
        put(gz_ref, slice(G_W + n * BRANCH_W, G_W + (n + 1) * BRANCH_W), (z / (1.0 + jnp.exp(-z))).astype(_BF16))
        c0 += BRANCH_W
    for n in range(G_W // BRANCH_W):
        g = jnp.dot(h, wq_ref[:, c0:c0 + BRANCH_W], preferred_element_type=_F32)
        put(gz_ref, slice(n * BRANCH_W, (n + 1) * BRANCH_W), (1.0 / (1.0 + jnp.exp(-g))).astype(_BF16))
        c0 += BRANCH_W


def _project(x, mod, norm_w, tab, weights, *, tile, with_q, name, nb=1):
    B, N, D = x.shape
    assert B % nb == 0 and (nb == 1 or (mod.shape[0] == 1 and tab is None))
    widths = [N_PAIRS * MLA_QK_W, BRANCH_W + 4 * LANES]
    if with_q:
        widths += [N_PAIRS * MLA_QK_W + 2 * BRANCH_W, G_W + Z_W]
    in_specs = [pl.BlockSpec((nb, tile, D), lambda j, b: (b, j, 0)),
                _const_spec(mod.shape),
                _const_spec(norm_w.shape)]
    args = [x, mod, norm_w]
    if tab is not None:
        in_specs.append(pl.BlockSpec((tile, TAB_W), lambda j, b: (j, 0)))
        args.append(tab)
    in_specs += [_const_spec(w.shape) for w in weights]
    out_specs = [pl.BlockSpec((nb, tile, w), lambda j, b: (b, j, 0)) for w in widths]
    out_shape = [jax.ShapeDtypeStruct((B, N, w), _BF16) for w in widths]
    if tab is not None:
        out_specs[0] = pl.BlockSpec((1, widths[0], tile), lambda j, b: (b, 0, j))
        out_shape[0] = jax.ShapeDtypeStruct((B, widths[0], N), _BF16)
    return pl.pallas_call(
        functools.partial(_proj_kernel, has_rope=tab is not None, with_q=with_q, per_row_mod=mod.shape[0] == B),
        grid=(N // tile, B // nb),
        in_specs=in_specs,
        out_specs=out_specs,
        out_shape=out_shape,
        compiler_params=_params(2),
        name=name,
    )(*args, *weights)


def _scores(q, k):
    return lax.dot_general(q, k, (((1,), (1,)), ((), ())), preferred_element_type=_F32)


def _stack_heads(q, mla):
    lane = lax.broadcasted_iota(jnp.int32, q.shape, 1)
    is_a = lane < HEAD_DIM
    if mla:
        is_a = is_a | ((lane >= LANES) & (lane < LANES + MLA_ROPE))
    zero = jnp.zeros_like(q)
    return jnp.concatenate([jnp.where(is_a, q, zero), jnp.where(is_a, zero, q)], axis=0)


def _fill_v1(v1_scr, v_ref, bi, n_blocks):
    n = v_ref.shape[1]
    for j in range(n_blocks):
        v1_scr[j, :, :LANES] = v_ref[bi, :, j * LANES:(j + 1) * LANES]
        v1_scr[j, :, LANES:] = jnp.ones((n, LANES), _BF16)


def _online_softmax_pv(q2, key_tiles):
    m = acc = None
    for k, v1, _, k_is_transposed in key_tiles:
        s = jnp.dot(q2, k, preferred_element_type=_F32) if k_is_transposed else _scores(q2, k)
        m_new = jnp.max(s, axis=-1, keepdims=True)
        if m is not None:
            m_new = jnp.maximum(m, m_new)
        pv = jnp.dot(jnp.exp2(s - m_new).astype(_BF16), v1, preferred_element_type=_F32)
        acc = pv if acc is None else jnp.exp2(m - m_new) * acc + pv
        m = m_new
    return acc[:, :LANES] / acc[:, LANES:]


def _sink_softmax_pv(q2, key_tiles, sink):
    scores = []
    m = sink
    for k, _, bias, _ in key_tiles:
        s = _scores(q2, k)
        if bias is not None:
            s = s + bias
        scores.append(s)
        m = jnp.maximum(m, jnp.max(s, axis=-1, keepdims=True))
    acc = None
    for s, (_, v1, _, _) in zip(scores, key_tiles):
        pv = jnp.dot(jnp.exp2(s - m).astype(_BF16), v1, preferred_element_type=_F32)
        acc = pv if acc is None else acc + pv
    return acc[:, :LANES] / (acc[:, LANES:] + jnp.exp2(sink - m))


def _gate_store(o_ref, z_ref, bi, rows, p, r):
    n = r.shape[0] // 2
    lane = lax.broadcasted_iota(jnp.int32, (n, LANES), 1)
    o = jnp.where(lane < HEAD_DIM, r[:n], r[n:])
    blk = slice(p * LANES, (p + 1) * LANES)
    o_ref[bi, rows, blk] = (o * z_ref[bi, rows, blk].astype(_F32)).astype(_BF16)


def _tile_rows(tile):
    return pl.ds(pl.multiple_of(tile * Q_TILE, Q_TILE), Q_TILE)


def _attn_kernel(*refs, mla, window, latent_keys):
    refs = list(refs)
    sink_ref = refs.pop(0) if window else None
    q_ref, z_ref, kc_ref, vc_ref = refs[:4]
    k_ref, v_ref = refs[4:6] if latent_keys else (None, None)
    o_ref, vc1_scr = refs[-2 - latent_keys], refs[-1 - latent_keys]
    v1_scr = refs[-1] if latent_keys else None
    for bi in range(q_ref.shape[0]):
        _attn_row(bi, sink_ref, q_ref, z_ref, kc_ref, vc_ref, k_ref, v_ref, o_ref, vc1_scr, v1_scr,
                  mla=mla, window=window)


def _attn_row(bi, sink_ref, q_ref, z_ref, kc_ref, vc_ref, k_ref, v_ref, o_ref, vc1_scr, v1_scr, *, mla, window):
    latent_keys = k_ref is not None
    n_tiles = q_ref.shape[1] // Q_TILE
    wq = MLA_QK_W if mla else LANES
    n_blocks = N_PAIRS if mla else 1
    vc1_scr = vc1_scr.at[bi]
    _fill_v1(vc1_scr, vc_ref, bi, n_blocks)
    if latent_keys:
        _fill_v1(v1_scr, v_ref, bi, n_blocks)
        n_lat = v_ref.shape[1]

    def ctx_tile(p):
        kb = slice(p * wq, (p + 1) * wq) if mla else slice(None)
        return (kc_ref[bi, :, kb], vc1_scr[p if mla else 0], None, False)

    def latent_tile(p, start, bias):
        keys = pl.ds(start, KEY_TILE)
        if mla:
            return (k_ref[bi, p * wq:(p + 1) * wq, keys], v1_scr[p, keys, :], bias, True)
        return (k_ref[bi, keys, :], v1_scr[0, keys, :], bias, False)

    def pair_sink(p):
        row = lax.broadcasted_iota(jnp.int32, (2 * Q_TILE, 1), 0)
        return jnp.where(row < Q_TILE, sink_ref[p], sink_ref[N_PAIRS + p]) * LOG2E

    def step(t, carry):
        if window and latent_keys:
            q0 = t * Q_TILE
            k0 = jnp.clip(q0 - WINDOW, 0, n_lat - BAND)
            shape = (2 * Q_TILE, KEY_TILE)
            col = lax.broadcasted_iota(jnp.int32, shape, 1)
            row = lax.broadcasted_iota(jnp.int32, shape, 0) & (Q_TILE - 1)
            lat = [(pl.multiple_of(k0 + b0, LANES),
                    jnp.where(jnp.abs((col + (k0 + b0)) - (row + q0)) <= WINDOW, 0.0, NEG_INF))
                   for b0 in range(0, BAND, KEY_TILE)]
        elif latent_keys:
            lat = [(k0, None) for k0 in range(0, n_lat, KEY_TILE)]
        else:
            lat = []
        rows = _tile_rows(t)
        for p in range(N_PAIRS):
            tiles = [latent_tile(p, start, bias) for start, bias in lat] + [ctx_tile(p)]
            q2 = _stack_heads(q_ref[bi, rows, p * wq:(p + 1) * wq], mla)
            r = _sink_softmax_pv(q2, tiles, pair_sink(p)) if window else _online_softmax_pv(q2, tiles)
            _gate_store(o_ref, z_ref, bi, rows, p, r)
        return carry

    lax.fori_loop(0, n_tiles, step, 0, unroll=min(n_tiles, ATTN_UNROLL))


def _attention(q, z, kc, vc, k=None, v=None, *, mla=False, sink=None, name, nb=1):
    B, nq, _ = q[0].shape
    latent_keys = k is not None
    assert B % nb == 0 and (nb == 1 or not latent_keys)

    def view(operand):
        a, width, blk = operand
        if width is None:
            return pl.BlockSpec((nb,) + a.shape[1:], lambda b: (b, 0, 0))
        return pl.BlockSpec((nb, a.shape[1], width), lambda b: (b, 0, blk))

    operands = [q, z, kc, vc] + ([k, v] if latent_keys else [])
    args = [o[0] for o in operands]
    in_specs = [view(o) for o in operands]
    if sink is not None:
        args.insert(0, sink)
        in_specs.insert(0, pl.BlockSpec(memory_space=pltpu.SMEM))
    n_blocks = N_PAIRS if mla else 1
    scratch = [pltpu.VMEM((nb, n_blocks, CTX_LEN, 2 * LANES), _BF16)]
    if latent_keys:
        scratch.append(pltpu.VMEM((n_blocks, v[0].shape[1], 2 * LANES), _BF16))
    return pl.pallas_call(
        functools.partial(_attn_kernel, mla=mla, window=sink is not None, latent_keys=latent_keys),
        grid=(B // nb,),
        in_specs=in_specs,
        out_specs=pl.BlockSpec((nb, nq, BRANCH_W), lambda b: (b, 0, 0)),
        out_shape=jax.ShapeDtypeStruct((B, nq, BRANCH_W), _BF16),
        scratch_shapes=scratch,
        compiler_params=_params(1),
        name=name,
    )(*args)


def _merge_kernel(um_ref, us_ref, ua_ref, sg_ref, wom_ref, wos_ref, woa_ref, wout_ref,
                  x_ref, mod_ref, *rest, final_norm, per_row_mod):
    nb, tile, _ = x_ref.shape
    rows = nb * tile
    y = None
    for n, (u_ref, w_ref) in enumerate(((um_ref, wom_ref), (us_ref, wos_ref), (ua_ref, woa_ref))):
        yb = jnp.dot(u_ref[...].reshape(rows, BRANCH_W), w_ref[...], preferred_element_type=_F32)
        yb = yb * sg_ref[:, :, n * D_MODEL:(n + 1) * D_MODEL].reshape(rows, D_MODEL).astype(_F32)
        y = yb if y is None else y + yb
    out = jnp.dot(y.astype(_BF16), wout_ref[...], preferred_element_type=_F32)
    mod = mod_ref[pl.program_id(0) if per_row_mod else 0]
    xn = x_ref[...].reshape(rows, D_MODEL) + mod[:, 2 * D_MODEL:] * out
    if final_norm:
        fnw_ref, o_ref = rest
        o_ref[...] = _rms(xn, fnw_ref[...]).reshape(nb, tile, D_MODEL)
    else:
        (o_ref,) = rest
        o_ref[...] = xn.reshape(nb, tile, D_MODEL)


def _merge(us, gz, weights, x, mod, fnw=None, *, tile, name, nb=1):
    B, N, D = x.shape
    assert B % nb == 0 and (nb == 1 or mod.shape[0] == 1)
    tok = lambda w: pl.BlockSpec((nb, tile, w), lambda b, j: (b, j, 0))
    in_specs = ([tok(BRANCH_W)] * 3 + [tok(G_W)] + [_const_spec(w.shape) for w in weights]
                + [tok(D), _const_spec(mod.shape)])
    args = [*us, gz, *weights, x, mod]
    if fnw is not None:
        in_specs.append(_const_spec(fnw.shape))
        args.append(fnw)
    return pl.pallas_call(
        functools.partial(_merge_kernel, final_norm=fnw is not None, per_row_mod=mod.shape[0] == B),
        grid=(B // nb, N // tile),
        in_specs=in_specs,
        out_specs=tok(D),
        out_shape=jax.ShapeDtypeStruct((B, N, D), _F32),
        compiler_params=_params(2),
        name=name,
    )(*args)


def _axial_tables(d, n_lat):
    h = d // 2
    t = jnp.arange(n_lat, dtype=jnp.int32)
    pos_row = (t // GRID_W).astype(_F32)
    pos_col = (t % GRID_W).astype(_F32)
    freqs = ROPE_THETA ** (-jnp.arange(0, h, 2, dtype=_F32) / h)
    ang_r = pos_row[:, None] * freqs[None, :]
    ang_c = pos_col[:, None] * freqs[None, :]
    cr, sr, cc, sc = jnp.cos(ang_r), jnp.sin(ang_r), jnp.cos(ang_c), jnp.sin(ang_c)
    z = jnp.zeros_like(sr)
    cos = jnp.concatenate([cr, cr, cc, cc], axis=-1)
    sinm = jnp.concatenate([-sr, z, -sc, z], axis=-1)
    sinp = jnp.concatenate([z, sr, z, sc], axis=-1)
    return cos, sinm, sinp


def _rope_tables(n_lat):
    set64 = [jnp.tile(t, (1, 2)) for t in _axial_tables(HEAD_DIM, n_lat)]
    pad = LANES - 2 * MLA_ROPE
    set32 = [jnp.concatenate([t, t, jnp.full((n_lat, pad), f, _F32)], axis=1)
             for t, f in zip(_axial_tables(MLA_ROPE, n_lat), (1.0, 0.0, 0.0))]
    return jnp.concatenate(set64 + set32, axis=1)


def _permute_heads(w, axis):
    shape = w.shape
    w = w.reshape(shape[:axis] + (KV_HEADS, HEADS // KV_HEADS, HEAD_DIM) + shape[axis + 1:])
    return jnp.swapaxes(w, axis, axis + 1).reshape(shape)


def _layer_weights(w_in, w_uq, w_ukv, w_o_swa, w_o_ax):
    D = w_in.shape[0]
    offs = {}
    o = 0
    for name, w in (("ckv", MLA_KV_RANK), ("kr", MLA_ROPE), ("sk", LANES), ("sv", LANES),
                    ("ak", LANES), ("av", LANES), ("cq", MLA_Q_RANK), ("sq", BRANCH_W),
                    ("aq", BRANCH_W), ("zm", BRANCH_W), ("zs", BRANCH_W), ("za", BRANCH_W),
                    ("g", G_W)):
        offs[name] = w_in[:, o:o + w]
        o += w
    kr = offs["kr"]
    kr2 = jnp.concatenate([kr, kr, jnp.zeros((D, LANES - 2 * MLA_ROPE), w_in.dtype)], axis=1)
    wkv = jnp.concatenate([offs["ckv"], kr2, offs["sk"], offs["sv"], offs["ak"], offs["av"]], axis=1)
    wq = jnp.concatenate([offs["cq"], _permute_heads(offs["sq"], 1), _permute_heads(offs["aq"], 1),
                          offs["zm"], _permute_heads(offs["zs"], 1), _permute_heads(offs["za"], 1),
                          offs["g"]], axis=1)
    uq = w_uq.reshape(MLA_Q_RANK, N_PAIRS, 2, HEAD_DIM + MLA_ROPE)
    nope = uq[..., :HEAD_DIM].reshape(MLA_Q_RANK, N_PAIRS, 2 * HEAD_DIM)
    rope = uq[..., HEAD_DIM:].reshape(MLA_Q_RANK, N_PAIRS, 2 * MLA_ROPE)
    pad = jnp.zeros((MLA_Q_RANK, N_PAIRS, MLA_QK_W - 2 * HEAD_DIM - 2 * MLA_ROPE), w_uq.dtype)
    wuq = jnp.concatenate([nope, rope, pad], axis=-1).reshape(MLA_Q_RANK, N_PAIRS * MLA_QK_W)
    ukv = w_ukv.reshape(MLA_KV_RANK, HEADS, 2, HEAD_DIM)
    wukv = jnp.concatenate([ukv[:, :, 0].reshape(MLA_KV_RANK, BRANCH_W),
                            ukv[:, :, 1].reshape(MLA_KV_RANK, BRANCH_W)], axis=1)
    bf = lambda a: a.astype(_BF16)
    return (bf(wkv), bf(wq), bf(wuq), bf(wukv),
            bf(_permute_heads(w_o_swa, 0)), bf(_permute_heads(w_o_ax, 0)))


_GZ_Z_BLK = G_W // BRANCH_W
_VKV_KV_BLK = BRANCH_W // LANES


def _attend_all(sink, pq, pc, px=None, *, tag, nb=1):
    q, gz = pq[2], pq[3]
    z = lambda branch: (gz, BRANCH_W, _GZ_Z_BLK + branch)
    kv = lambda p, i: None if p is None else (p[1], LANES, _VKV_KV_BLK + i)
    mla_k = None if px is None else (px[0], None, 0)
    mla_v = None if px is None else (px[1], BRANCH_W, 0)
    return (_attention((q, N_PAIRS * MLA_QK_W, 0), z(0), (pc[0], None, 0), (pc[1], BRANCH_W, 0), mla_k, mla_v,
                       mla=True, name="attn_mla" + tag, nb=nb),
            _attention((q, BRANCH_W, 2), z(1), kv(pc, 0), kv(pc, 1), kv(px, 0), kv(px, 1),
                       sink=sink, name="attn_window" + tag, nb=nb),
            _attention((q, BRANCH_W, 3), z(2), kv(pc, 2), kv(pc, 3), kv(px, 2), kv(px, 3),
                       name="attn_axial" + tag, nb=nb))


def kernel(x, c, ctx, c_ctx, ada_w, ada_b, norm_w, w_in, mla_q_norm, mla_w_uq, mla_kv_norm, mla_w_ukv,
           swa_sink, ax_q_norm, ax_k_norm, w_o_mla, w_o_swa, w_o_ax, w_out, final_norm_w):
    B, T, D = x.shape
    depth = w_in.shape[0]
    merge_tile = min(MERGE_TILE, T)
    ctx_nb = CTX_ROWS_PER_STEP if B % CTX_ROWS_PER_STEP == 0 else 1
    ctx_proj_nb = CTX_PROJ_ROWS_PER_STEP if B % CTX_PROJ_ROWS_PER_STEP == 0 else 1
    assert (D, ctx.shape[1]) == (D_MODEL, CTX_LEN) and T % TOK_TILE == 0 and T % merge_tile == 0 and T >= BAND

    mod_rows = -(-(B + 1) // 8) * 8
    cc = jnp.concatenate([c, c_ctx[None, :], jnp.zeros((mod_rows - B - 1, D), c.dtype)], axis=0)
    mod = _modulation(cc, ada_w, ada_b)
    tab = _rope_tables(T)
    fnw = final_norm_w.reshape(1, D)
    two_heads = lambda w: jnp.tile(w, 2).reshape(1, LANES)

    for l in range(depth):
        update_ctx = l < depth - 1
        wkv, wq, wuq, wukv, wos, woa = _layer_weights(w_in[l], mla_w_uq[l], mla_w_ukv[l], w_o_swa[l], w_o_ax[l])
        proj_w = (wkv, wq, wuq, wukv, mla_q_norm[l].reshape(1, -1), mla_kv_norm[l].reshape(1, -1),
                  two_heads(ax_q_norm[l]), two_heads(ax_k_norm[l]))
        merge_w = (w_o_mla[l].astype(_BF16), wos, woa, w_out[l].astype(_BF16))
        nw = norm_w[l].reshape(1, D)
        modx = mod[l, :B].reshape(B, 1, 3 * D)
        modc = mod[l, B:B + 1].reshape(1, 1, 3 * D)

        pc = _project(ctx, modc, nw, None, proj_w, tile=CTX_LEN, with_q=update_ctx, name="project_ctx", nb=ctx_proj_nb)
        px = _project(x, modx, nw, tab, proj_w, tile=TOK_TILE, with_q=True, name="project")
        u3 = _attend_all(swa_sink[l], px, pc, px, tag="")
        if update_ctx:
            uc3 = _attend_all(swa_sink[l], pc, pc, tag="_ctx", nb=ctx_nb)
            ctx = _merge(uc3, pc[3], merge_w, ctx, modc, tile=CTX_LEN, name="merge_ctx", nb=ctx_nb)
            x = _merge(u3, px[3], merge_w, x, modx, tile=merge_tile, name="merge")
        else:
            x = _merge(u3, px[3], merge_w, x, modx, fnw, tile=merge_tile, name="merge_final")
    return x
```

```python
import functools

import jax
import jax.numpy as jnp
from jax import lax
from jax.experimental import pallas as pl
from jax.experimental.pallas import tpu as pltpu

D_MODEL = 1024
CTX_LEN = 256
GRID_W = 64
ROPE_THETA = 10000.0
RMS_EPS = 1e-6
NEG_INF = -1e30
N_BRANCH = 3
WINDOW = 128

HEADS = 8
KV_HEADS = 2
HEAD_DIM = 64
MLA_ROPE = 32
MLA_Q_RANK = 384
MLA_KV_RANK = 256
LOG2E = 1.4426950408889634
MLA_SCALE = (HEAD_DIM + MLA_ROPE) ** -0.5 * LOG2E
GQA_SCALE = HEAD_DIM ** -0.5 * LOG2E
BRANCH_W = HEADS * HEAD_DIM

LANES = 128
N_PAIRS = BRANCH_W // LANES
MLA_QK_W = 2 * LANES

TOK_TILE = 512
MERGE_TILE = 1024
CTX_ROWS_PER_STEP = 4
CTX_PROJ_ROWS_PER_STEP = 2
Q_TILE = 256
BAND = Q_TILE + 2 * WINDOW
KEY_TILE = 256
ATTN_UNROLL = 4

Z_W = N_BRANCH * BRANCH_W
G_W = N_BRANCH * D_MODEL
TAB_W = 6 * LANES

VMEM_LIMIT = 56 * 1024 * 1024

_F32 = jnp.float32
_BF16 = jnp.bfloat16


def _params(n_axes):
    return pltpu.CompilerParams(
        dimension_semantics=("arbitrary",) * n_axes, vmem_limit_bytes=VMEM_LIMIT)


def _const_spec(shape):
    nd = len(shape)
    return pl.BlockSpec(shape, lambda *_: (0,) * nd, pipeline_mode=pl.Buffered(1))


def _mod_kernel(c_ref, w_ref, b_ref, o_ref):
    c = c_ref[...]
    s = c / (1.0 + jnp.exp(-c))
    o_ref[0] = jnp.dot(s, w_ref[0], preferred_element_type=_F32,
                       precision=lax.Precision.HIGHEST) + b_ref[0]


def _modulation(cc, ada_w, ada_b):
    depth = ada_w.shape[0]
    rows = cc.shape[0]
    col_tile = D_MODEL
    return pl.pallas_call(
        _mod_kernel,
        grid=(depth, (3 * D_MODEL) // col_tile),
        in_specs=[
            pl.BlockSpec((rows, D_MODEL), lambda l, n: (0, 0)),
            pl.BlockSpec((1, D_MODEL, col_tile), lambda l, n: (l, 0, n)),
            pl.BlockSpec((1, 1, col_tile), lambda l, n: (l, 0, n)),
        ],
        out_specs=pl.BlockSpec((1, rows, col_tile), lambda l, n: (l, 0, n)),
        out_shape=jax.ShapeDtypeStruct((depth, rows, 3 * D_MODEL), _F32),
        compiler_params=_params(2),
        name="modulation",
    )(cc, ada_w, ada_b.reshape(depth, 1, 3 * D_MODEL))


def _rms(x, w):
    ms = jnp.mean(x * x, axis=-1, keepdims=True)
    return x * lax.rsqrt(ms + RMS_EPS) * w


def _head_rms(x, w):
    lane = lax.broadcasted_iota(jnp.int32, x.shape, 1)
    y = x * x
    s = 1
    while s < HEAD_DIM:
        up = pltpu.roll(y, s, 1)
        dn = pltpu.roll(y, LANES - s, 1)
        y = y + jnp.where((lane & s) != 0, up, dn)
        s *= 2
    return x * lax.rsqrt(y * (1.0 / HEAD_DIM) + RMS_EPS) * w


def _rope(x, tab_ref, wide):
    if tab_ref is None:
        return x
    k, r = (0, HEAD_DIM // 4) if wide else (1, MLA_ROPE // 4)
    cos = tab_ref[:, (3 * k) * LANES:(3 * k + 1) * LANES]
    sinm = tab_ref[:, (3 * k + 1) * LANES:(3 * k + 2) * LANES]
    sinp = tab_ref[:, (3 * k + 2) * LANES:(3 * k + 3) * LANES]
    return x * cos + pltpu.roll(x, LANES - r, 1) * sinm + pltpu.roll(x, r, 1) * sinp


def _proj_kernel(*refs, has_rope, with_q, per_row_mod):
    refs = list(refs)
    x_ref, mod_ref, nw_ref = refs[:3]
    del refs[:3]
    tab_ref = refs.pop(0) if has_rope else None
    wkv_ref, wq_ref, wuq_ref, wukv_ref, qn_ref, kvn_ref, axq_ref, axk_ref = refs[:8]
    km_ref, vkv_ref = refs[8:10]

    nb, tile, _ = x_ref.shape

    def put(ref, cols, val):
        ref[:, :, cols] = val.reshape(nb, tile, val.shape[-1])

    mod = mod_ref[pl.program_id(1) if per_row_mod else 0]
    x = x_ref[...].reshape(nb * tile, D_MODEL)
    gain = nw_ref[...] * (1.0 + mod[:, D_MODEL:2 * D_MODEL])
    inv = lax.rsqrt(jnp.mean(x * x, axis=-1, keepdims=True) + RMS_EPS)
    h = (x * inv * gain + mod[:, :D_MODEL]).astype(_BF16)

    pkv = jnp.dot(h, wkv_ref[...], preferred_element_type=_F32)
    ckvn = _rms(pkv[:, :MLA_KV_RANK], kvn_ref[...]).astype(_BF16)
    kvm = jnp.dot(ckvn, wukv_ref[...], preferred_element_type=_F32)
    put(vkv_ref, slice(0, BRANCH_W), kvm[:, BRANCH_W:].astype(_BF16))
    o = MLA_KV_RANK
    kr2 = _rope(pkv[:, o:o + LANES], tab_ref, False).astype(_BF16)
    for p in range(N_PAIRS):
        kn = kvm[:, p * LANES:(p + 1) * LANES]
        lo, hi = p * MLA_QK_W, p * MLA_QK_W + LANES
        if has_rope:
            km_ref[0, lo:hi, :] = kn.T.astype(_BF16)
            km_ref[0, hi:hi + LANES, :] = kr2.T
        else:
            put(km_ref, slice(lo, hi), kn.astype(_BF16))
            put(km_ref, slice(hi, hi + LANES), kr2)
    o += LANES
    put(vkv_ref, slice(BRANCH_W, BRANCH_W + LANES), _rope(pkv[:, o:o + LANES], tab_ref, True).astype(_BF16))
    o += LANES
    put(vkv_ref, slice(BRANCH_W + LANES, BRANCH_W + 2 * LANES), pkv[:, o:o + LANES].astype(_BF16))
    o += LANES
    put(vkv_ref, slice(BRANCH_W + 2 * LANES, BRANCH_W + 3 * LANES),
        _rope(_head_rms(pkv[:, o:o + LANES], axk_ref[...]), tab_ref, True).astype(_BF16))
    o += LANES
    put(vkv_ref, slice(BRANCH_W + 3 * LANES, BRANCH_W + 4 * LANES), pkv[:, o:o + LANES].astype(_BF16))
    if not with_q:
        return

    q_ref, gz_ref = refs[10:]
    cq = jnp.dot(h, wq_ref[:, :MLA_Q_RANK], preferred_element_type=_F32)
    cqn = _rms(cq, qn_ref[...]).astype(_BF16)
    qm = jnp.dot(cqn, wuq_ref[...], preferred_element_type=_F32)
    for p in range(N_PAIRS):
        lo = p * MLA_QK_W
        put(q_ref, slice(lo, lo + LANES), (qm[:, lo:lo + LANES] * MLA_SCALE).astype(_BF16))
        put(q_ref, slice(lo + LANES, lo + MLA_QK_W),
            (_rope(qm[:, lo + LANES:lo + MLA_QK_W], tab_ref, False) * MLA_SCALE).astype(_BF16))
    c0 = MLA_Q_RANK
    sq = jnp.dot(h, wq_ref[:, c0:c0 + BRANCH_W], preferred_element_type=_F32)
    c0 += BRANCH_W
    aq = jnp.dot(h, wq_ref[:, c0:c0 + BRANCH_W], preferred_element_type=_F32)
    c0 += BRANCH_W
    for p in range(N_PAIRS):
        blk = slice(p * LANES, (p + 1) * LANES)
        qs0 = N_PAIRS * MLA_QK_W + p * LANES
        put(q_ref, slice(qs0, qs0 + LANES), (_rope(sq[:, blk], tab_ref, True) * GQA_SCALE).astype(_BF16))
        aqn = _head_rms(aq[:, blk], axq_ref[...])
        put(q_ref, slice(qs0 + BRANCH_W, qs0 + BRANCH_W + LANES), (_rope(aqn, tab_ref, True) * GQA_SCALE).astype(_BF16))
    for n in range(Z_W // BRANCH_W):
        z = jnp.dot(h, wq_ref[:, c0:c0 + BRANCH_W], preferred_element_type=_F32)
        put(gz_ref, slice(G_W + n * BRANCH_W, G_W + (n + 1) * BRANCH_W), (z / (1.0 + jnp.exp(-z))).astype(_BF16))
        c0 += BRANCH_W
    for n in range(G_W // BRANCH_W):
        g = jnp.dot(h, wq_ref[:, c0:c0 + BRANCH_W], preferred_element_type=_F32)
        put(gz_ref, slice(n * BRANCH_W, (n + 1) * BRANCH_W), g.astype(_BF16))
        c0 += BRANCH_W


def _project(x, mod, norm_w, tab, weights, *, tile, with_q, name, nb=1):
    B, N, D = x.shape
    assert B % nb == 0 and (nb == 1 or (mod.shape[0] == 1 and tab is None))
    widths = [N_PAIRS * MLA_QK_W, BRANCH_W + 4 * LANES]
    if with_q:
        widths += [N_PAIRS * MLA_QK_W + 2 * BRANCH_W, G_W + Z_W]
    in_specs = [pl.BlockSpec((nb, tile, D), lambda j, b: (b, j, 0)),
                _const_spec(mod.shape),
                _const_spec(norm_w.shape)]
    args = [x, mod, norm_w]
    if tab is not None:
        in_specs.append(pl.BlockSpec((tile, TAB_W), lambda j, b: (j, 0)))
        args.append(tab)
    in_specs += [_const_spec(w.shape) for w in weights]
    out_specs = [pl.BlockSpec((nb, tile, w), lambda j, b: (b, j, 0)) for w in widths]
    out_shape = [jax.ShapeDtypeStruct((B, N, w), _BF16) for w in widths]
    if tab is not None:
        out_specs[0] = pl.BlockSpec((1, widths[0], tile), lambda j, b: (b, 0, j))
        out_shape[0] = jax.ShapeDtypeStruct((B, widths[0], N), _BF16)
    return pl.pallas_call(
        functools.partial(_proj_kernel, has_rope=tab is not None, with_q=with_q, per_row_mod=mod.shape[0] == B),
        grid=(N // tile, B // nb),
        in_specs=in_specs,
        out_specs=out_specs,
        out_shape=out_shape,
        compiler_params=_params(2),
        name=name,
    )(*args, *weights)


def _scores(q, k):
    return lax.dot_general(q, k, (((1,), (1,)), ((), ())), preferred_element_type=_F32)


def _stack_heads(q, mla):
    lane = lax.broadcasted_iota(jnp.int32, q.shape, 1)
    is_a = lane < HEAD_DIM
    if mla:
        is_a = is_a | ((lane >= LANES) & (lane < LANES + MLA_ROPE))
    zero = jnp.zeros_like(q)
    return jnp.concatenate([jnp.where(is_a, q, zero), jnp.where(is_a, zero, q)], axis=0)


def _fill_v1(v1_scr, v_ref, bi, n_blocks):
    n = v_ref.shape[1]
    for j in range(n_blocks):
        v1_scr[j, :, :LANES] = v_ref[bi, :, j * LANES:(j + 1) * LANES]
        v1_scr[j, :, LANES:] = jnp.ones((n, LANES), _BF16)


def _online_softmax_pv(q2, key_tiles):
    m = acc = None
    for k, v1, _, k_is_transposed in key_tiles:
        s = jnp.dot(q2, k, preferred_element_type=_F32) if k_is_transposed else _scores(q2, k)
        m_new = jnp.max(s, axis=-1, keepdims=True)
        if m is not None:
            m_new = jnp.maximum(m, m_new)
        pv = jnp.dot(jnp.exp2(s - m_new).astype(_BF16), v1, preferred_element_type=_F32)
        acc = pv if acc is None else jnp.exp2(m - m_new) * acc + pv
        m = m_new
    return acc[:, :LANES] / acc[:, LANES:]


def _sink_softmax_pv(q2, key_tiles, sink):
    scores = []
    m = sink
    for k, _, bias, _ in key_tiles:
        s = _scores(q2, k)
        if bias is not None:
            s = s + bias
        scores.append(s)
        m = jnp.maximum(m, jnp.max(s, axis=-1, keepdims=True))
    acc = None
    for s, (_, v1, _, _) in zip(scores, key_tiles):
        pv = jnp.dot(jnp.exp2(s - m).astype(_BF16), v1, preferred_element_type=_F32)
        acc = pv if acc is None else acc + pv
    return acc[:, :LANES] / (acc[:, LANES:] + jnp.exp2(sink - m))


def _gate_store(o_ref, z_ref, bi, rows, p, r):
    n = r.shape[0] // 2
    lane = lax.broadcasted_iota(jnp.int32, (n, LANES), 1)
    o = jnp.where(lane < HEAD_DIM, r[:n], r[n:])
    blk = slice(p * LANES, (p + 1) * LANES)
    o_ref[bi, rows, blk] = (o * z_ref[bi, rows, blk].astype(_F32)).astype(_BF16)


def _tile_rows(tile):
    return pl.ds(pl.multiple_of(tile * Q_TILE, Q_TILE), Q_TILE)


def _attn_kernel(*refs, mla, window, latent_keys):
    refs = list(refs)
    sink_ref = refs.pop(0) if window else None
    q_ref, z_ref, kc_ref, vc_ref = refs[:4]
    k_ref, v_ref = refs[4:6] if latent_keys else (None, None)
    o_ref, vc1_scr = refs[-2 - latent_keys], refs[-1 - latent_keys]
    v1_scr = refs[-1] if latent_keys else None
    for bi in range(q_ref.shape[0]):
        _attn_row(bi, sink_ref, q_ref, z_ref, kc_ref, vc_ref, k_ref, v_ref, o_ref, vc1_scr, v1_scr,
                  mla=mla, window=window)


def _attn_row(bi, sink_ref, q_ref, z_ref, kc_ref, vc_ref, k_ref, v_ref, o_ref, vc1_scr, v1_scr, *, mla, window):
    latent_keys = k_ref is not None
    n_tiles = q_ref.shape[1] // Q_TILE
    wq = MLA_QK_W if mla else LANES
    n_blocks = N_PAIRS if mla else 1
    vc1_scr = vc1_scr.at[bi]
    _fill_v1(vc1_scr, vc_ref, bi, n_blocks)
    if latent_keys:
        _fill_v1(v1_scr, v_ref, bi, n_blocks)
        n_lat = v_ref.shape[1]

    def ctx_tile(p):
        kb = slice(p * wq, (p + 1) * wq) if mla else slice(None)
        return (kc_ref[bi, :, kb], vc1_scr[p if mla else 0], None, False)

    def latent_tile(p, start, bias):
        keys = pl.ds(start, KEY_TILE)
        if mla:
            return (k_ref[bi, p * wq:(p + 1) * wq, keys], v1_scr[p, keys, :], bias, True)
        return (k_ref[bi, keys, :], v1_scr[0, keys, :], bias, False)

    def pair_sink(p):
        row = lax.broadcasted_iota(jnp.int32, (2 * Q_TILE, 1), 0)
        return jnp.where(row < Q_TILE, sink_ref[p], sink_ref[N_PAIRS + p]) * LOG2E

    def step(t, carry):
        if window and latent_keys:
            q0 = t * Q_TILE
            k0 = jnp.clip(q0 - WINDOW, 0, n_lat - BAND)
            shape = (2 * Q_TILE, KEY_TILE)
            col = lax.broadcasted_iota(jnp.int32, shape, 1)
            row = lax.broadcasted_iota(jnp.int32, shape, 0) & (Q_TILE - 1)
            lat = [(pl.multiple_of(k0 + b0, LANES),
                    jnp.where(jnp.abs((col + (k0 + b0)) - (row + q0)) <= WINDOW, 0.0, NEG_INF))
                   for b0 in range(0, BAND, KEY_TILE)]
        elif latent_keys:
            lat = [(k0, None) for k0 in range(0, n_lat, KEY_TILE)]
        else:
            lat = []
        rows = _tile_rows(t)
        for p in range(N_PAIRS):
            tiles = [latent_tile(p, start, bias) for start, bias in lat] + [ctx_tile(p)]
            q2 = _stack_heads(q_ref[bi, rows, p * wq:(p + 1) * wq], mla)
            r = _sink_softmax_pv(q2, tiles, pair_sink(p)) if window else _online_softmax_pv(q2, tiles)
            _gate_store(o_ref, z_ref, bi, rows, p, r)
        return carry

    lax.fori_loop(0, n_tiles, step, 0, unroll=min(n_tiles, ATTN_UNROLL))


def _attention(q, z, kc, vc, k=None, v=None, *, mla=False, sink=None, name, nb=1):
    B, nq, _ = q[0].shape
    latent_keys = k is not None
    assert B % nb == 0 and (nb == 1 or not latent_keys)

    def view(operand):
        a, width, blk = operand
        if width is None:
            return pl.BlockSpec((nb,) + a.shape[1:], lambda b: (b, 0, 0))
        return pl.BlockSpec((nb, a.shape[1], width), lambda b: (b, 0, blk))

    operands = [q, z, kc, vc] + ([k, v] if latent_keys else [])
    args = [o[0] for o in operands]
    in_specs = [view(o) for o in operands]
    if sink is not None:
        args.insert(0, sink)
        in_specs.insert(0, pl.BlockSpec(memory_space=pltpu.SMEM))
    n_blocks = N_PAIRS if mla else 1
    scratch = [pltpu.VMEM((nb, n_blocks, CTX_LEN, 2 * LANES), _BF16)]
    if latent_keys:
        scratch.append(pltpu.VMEM((n_blocks, v[0].shape[1], 2 * LANES), _BF16))
    return pl.pallas_call(
        functools.partial(_attn_kernel, mla=mla, window=sink is not None, latent_keys=latent_keys),
        grid=(B // nb,),
        in_specs=in_specs,
        out_specs=pl.BlockSpec((nb, nq, BRANCH_W), lambda b: (b, 0, 0)),
        out_shape=jax.ShapeDtypeStruct((B, nq, BRANCH_W), _BF16),
        scratch_shapes=scratch,
        compiler_params=_params(1),
        name=name,
    )(*args)


def _merge_kernel(um_ref, us_ref, ua_ref, sg_ref, wom_ref, wos_ref, woa_ref, wout_ref,
                  x_ref, mod_ref, *rest, final_norm, per_row_mod):
    nb, tile, _ = x_ref.shape
    rows = nb * tile
    y = None
    for n, (u_ref, w_ref) in enumerate(((um_ref, wom_ref), (us_ref, wos_ref), (ua_ref, woa_ref))):
        yb = jnp.dot(u_ref[...].reshape(rows, BRANCH_W), w_ref[...], preferred_element_type=_F32)
        g = sg_ref[:, :, n * D_MODEL:(n + 1) * D_MODEL].reshape(rows, D_MODEL).astype(_F32)
        yb = yb / (1.0 + jnp.exp(-g))
        y = yb if y is None else y + yb
    out = jnp.dot(y.astype(_BF16), wout_ref[...], preferred_element_type=_F32)
    mod = mod_ref[pl.program_id(0) if per_row_mod else 0]
    xn = x_ref[...].reshape(rows, D_MODEL) + mod[:, 2 * D_MODEL:] * out
    if final_norm:
        fnw_ref, o_ref = rest
        o_ref[...] = _rms(xn, fnw_ref[...]).reshape(nb, tile, D_MODEL)
    else:
        (o_ref,) = rest
        o_ref[...] = xn.reshape(nb, tile, D_MODEL)


def _merge(us, gz, weights, x, mod, fnw=None, *, tile, name, nb=1):
    B, N, D = x.shape
    assert B % nb == 0 and (nb == 1 or mod.shape[0] == 1)
    tok = lambda w: pl.BlockSpec((nb, tile, w), lambda b, j: (b, j, 0))
    in_specs = ([tok(BRANCH_W)] * 3 + [tok(G_W)] + [_const_spec(w.shape) for w in weights]
                + [tok(D), _const_spec(mod.shape)])
    args = [*us, gz, *weights, x, mod]
    if fnw is not None:
        in_specs.append(_const_spec(fnw.shape))
        args.append(fnw)
    return pl.pallas_call(
        functools.partial(_merge_kernel, final_norm=fnw is not None, per_row_mod=mod.shape[0] == B),
        grid=(B // nb, N // tile),
        in_specs=in_specs,
        out_specs=tok(D),
        out_shape=jax.ShapeDtypeStruct((B, N, D), _F32),
        compiler_params=_params(2),
        name=name,
    )(*args)


def _axial_tables(d, n_lat):
    h = d // 2
    t = jnp.arange(n_lat, dtype=jnp.int32)
    pos_row = (t // GRID_W).astype(_F32)
    pos_col = (t % GRID_W).astype(_F32)
    freqs = ROPE_THETA ** (-jnp.arange(0, h, 2, dtype=_F32) / h)
    ang_r = pos_row[:, None] * freqs[None, :]
    ang_c = pos_col[:, None] * freqs[None, :]
    cr, sr, cc, sc = jnp.cos(ang_r), jnp.sin(ang_r), jnp.cos(ang_c), jnp.sin(ang_c)
    z = jnp.zeros_like(sr)
    cos = jnp.concatenate([cr, cr, cc, cc], axis=-1)
    sinm = jnp.concatenate([-sr, z, -sc, z], axis=-1)
    sinp = jnp.concatenate([z, sr, z, sc], axis=-1)
    return cos, sinm, sinp


def _rope_tables(n_lat):
    set64 = [jnp.tile(t, (1, 2)) for t in _axial_tables(HEAD_DIM, n_lat)]
    pad = LANES - 2 * MLA_ROPE
    set32 = [jnp.concatenate([t, t, jnp.full((n_lat, pad), f, _F32)], axis=1)
             for t, f in zip(_axial_tables(MLA_ROPE, n_lat), (1.0, 0.0, 0.0))]
    return jnp.concatenate(set64 + set32, axis=1)


def _permute_heads(w, axis):
    shape = w.shape
    w = w.reshape(shape[:axis] + (KV_HEADS, HEADS // KV_HEADS, HEAD_DIM) + shape[axis + 1:])
    return jnp.swapaxes(w, axis, axis + 1).reshape(shape)


def _layer_weights(w_in, w_uq, w_ukv, w_o_swa, w_o_ax):
    D = w_in.shape[0]
    offs = {}
    o = 0
    for name, w in (("ckv", MLA_KV_RANK), ("kr", MLA_ROPE), ("sk", LANES), ("sv", LANES),
                    ("ak", LANES), ("av", LANES), ("cq", MLA_Q_RANK), ("sq", BRANCH_W),
                    ("aq", BRANCH_W), ("zm", BRANCH_W), ("zs", BRANCH_W), ("za", BRANCH_W),
                    ("g", G_W)):
        offs[name] = w_in[:, o:o + w]
        o += w
    kr = offs["kr"]
    kr2 = jnp.concatenate([kr, kr, jnp.zeros((D, LANES - 2 * MLA_ROPE), w_in.dtype)], axis=1)
    wkv = jnp.concatenate([offs["ckv"], kr2, offs["sk"], offs["sv"], offs["ak"], offs["av"]], axis=1)
    wq = jnp.concatenate([offs["cq"], _permute_heads(offs["sq"], 1), _permute_heads(offs["aq"], 1),
                          offs["zm"], _permute_heads(offs["zs"], 1), _permute_heads(offs["za"], 1),
                          offs["g"]], axis=1)
    uq = w_uq.reshape(MLA_Q_RANK, N_PAIRS, 2, HEAD_DIM + MLA_ROPE)
    nope = uq[..., :HEAD_DIM].reshape(MLA_Q_RANK, N_PAIRS, 2 * HEAD_DIM)
    rope = uq[..., HEAD_DIM:].reshape(MLA_Q_RANK, N_PAIRS, 2 * MLA_ROPE)
    pad = jnp.zeros((MLA_Q_RANK, N_PAIRS, MLA_QK_W - 2 * HEAD_DIM - 2 * MLA_ROPE), w_uq.dtype)
    wuq = jnp.concatenate([nope, rope, pad], axis=-1).reshape(MLA_Q_RANK, N_PAIRS * MLA_QK_W)
    ukv = w_ukv.reshape(MLA_KV_RANK, HEADS, 2, HEAD_DIM)
    wukv = jnp.concatenate([ukv[:, :, 0].reshape(MLA_KV_RANK, BRANCH_W),
                            ukv[:, :, 1].reshape(MLA_KV_RANK, BRANCH_W)], axis=1)
    bf = lambda a: a.astype(_BF16)
    return (bf(wkv), bf(wq), bf(wuq), bf(wukv),
            bf(_permute_heads(w_o_swa, 0)), bf(_permute_heads(w_o_ax, 0)))


_GZ_Z_BLK = G_W // BRANCH_W
_VKV_KV_BLK = BRANCH_W // LANES


def _attend_all(sink, pq, pc, px=None, *, tag, nb=1):
    q, gz = pq[2], pq[3]
    z = lambda branch: (gz, BRANCH_W, _GZ_Z_BLK + branch)
    kv = lambda p, i: None if p is None else (p[1], LANES, _VKV_KV_BLK + i)
    mla_k = None if px is None else (px[0], None, 0)
    mla_v = None if px is None else (px[1], BRANCH_W, 0)
    return (_attention((q, N_PAIRS * MLA_QK_W, 0), z(0), (pc[0], None, 0), (pc[1], BRANCH_W, 0), mla_k, mla_v,
                       mla=True, name="attn_mla" + tag, nb=nb),
            _attention((q, BRANCH_W, 2), z(1), kv(pc, 0), kv(pc, 1), kv(px, 0), kv(px, 1),
                       sink=sink, name="attn_window" + tag, nb=nb),
            _attention((q, BRANCH_W, 3), z(2), kv(pc, 2), kv(pc, 3), kv(px, 2), kv(px, 3),
                       name="attn_axial" + tag, nb=nb))


def kernel(x, c, ctx, c_ctx, ada_w, ada_b, norm_w, w_in, mla_q_norm, mla_w_uq, mla_kv_norm, mla_w_ukv,
           swa_sink, ax_q_norm, ax_k_norm, w_o_mla, w_o_swa, w_o_ax, w_out, final_norm_w):
    B, T, D = x.shape
    depth = w_in.shape[0]
    merge_tile = min(MERGE_TILE, T)
    ctx_nb = CTX_ROWS_PER_STEP if B % CTX_ROWS_PER_STEP == 0 else 1
    ctx_proj_nb = CTX_PROJ_ROWS_PER_STEP if B % CTX_PROJ_ROWS_PER_STEP == 0 else 1
    assert (D, ctx.shape[1]) == (D_MODEL, CTX_LEN) and T % TOK_TILE == 0 and T % merge_tile == 0 and T >= BAND

    mod_rows = -(-(B + 1) // 8) * 8
    cc = jnp.concatenate([c, c_ctx[None, :], jnp.zeros((mod_rows - B - 1, D), c.dtype)], axis=0)
    mod = _modulation(cc, ada_w, ada_b)
    tab = _rope_tables(T)
    fnw = final_norm_w.reshape(1, D)
    two_heads = lambda w: jnp.tile(w, 2).reshape(1, LANES)

    for l in range(depth):
        update_ctx = l < depth - 1
        wkv, wq, wuq, wukv, wos, woa = _layer_weights(w_in[l], mla_w_uq[l], mla_w_ukv[l], w_o_swa[l], w_o_ax[l])
        proj_w = (wkv, wq, wuq, wukv, mla_q_norm[l].reshape(1, -1), mla_kv_norm[l].reshape(1, -1),
                  two_heads(ax_q_norm[l]), two_heads(ax_k_norm[l]))
        merge_w = (w_o_mla[l].astype(_BF16), wos, woa, w_out[l].astype(_BF16))
        nw = norm_w[l].reshape(1, D)
        modx = mod[l, :B].reshape(B, 1, 3 * D)
        modc = mod[l, B:B + 1].reshape(1, 1, 3 * D)

        pc = _project(ctx, modc, nw, None, proj_w, tile=CTX_LEN, with_q=update_ctx, name="project_ctx", nb=ctx_proj_nb)
        px = _project(x, modx, nw, tab, proj_w, tile=TOK_TILE, with_q=True, name="project")
        u3 = _attend_all(swa_sink[l], px, pc, px, tag="")
        if update_ctx:
            uc3 = _attend_all(swa_sink[l], pc, pc, tag="_ctx", nb=ctx_nb)
            ctx = _merge(uc3, pc[3], merge_w, ctx, modc, tile=CTX_LEN, name="merge_ctx", nb=ctx_nb)
            x = _merge(u3, px[3], merge_w, x, modx, tile=merge_tile, name="merge")
        else:
            x = _merge(u3, px[3], merge_w, x, modx, fnw, tile=merge_tile, name="merge_final")
    return x
```

```python
import functools

import jax
import jax.numpy as jnp
from jax import lax
from jax.experimental import pallas as pl
from jax.experimental.pallas import tpu as pltpu

D_MODEL = 1024
CTX_LEN = 256
GRID_W = 64
ROPE_THETA = 10000.0
RMS_EPS = 1e-6
NEG_INF = -1e30
N_BRANCH = 3
WINDOW = 128

HEADS = 8
KV_HEADS = 2
HEAD_DIM = 64
MLA_ROPE = 32
MLA_Q_RANK = 384
MLA_KV_RANK = 256
LOG2E = 1.4426950408889634
MLA_SCALE = (HEAD_DIM + MLA_ROPE) ** -0.5 * LOG2E
GQA_SCALE = HEAD_DIM ** -0.5 * LOG2E
BRANCH_W = HEADS * HEAD_DIM

LANES = 128
N_PAIRS = BRANCH_W // LANES
MLA_QK_W = 2 * LANES

TOK_TILE = 512
MERGE_TILE = 1024
CTX_ROWS_PER_STEP = 4
CTX_PROJ_ROWS_PER_STEP = 2
Q_TILE = 256
BAND = Q_TILE + 2 * WINDOW
KEY_TILE = 256
ATTN_UNROLL = 4

Z_W = N_BRANCH * BRANCH_W
G_W = N_BRANCH * D_MODEL
TAB_W = 6 * LANES

VMEM_LIMIT = 56 * 1024 * 1024

_F32 = jnp.float32
_BF16 = jnp.bfloat16


def _params(n_axes):
    return pltpu.CompilerParams(
        dimension_semantics=("arbitrary",) * n_axes, vmem_limit_bytes=VMEM_LIMIT)


def _const_spec(shape):
    nd = len(shape)
    return pl.BlockSpec(shape, lambda *_: (0,) * nd, pipeline_mode=pl.Buffered(1))


def _mod_kernel(c_ref, w_ref, b_ref, o_ref):
    c = c_ref[...]
    s = c / (1.0 + jnp.exp(-c))
    o_ref[0] = jnp.dot(s, w_ref[0], preferred_element_type=_F32,
                       precision=lax.Precision.HIGHEST) + b_ref[0]


def _modulation(cc, ada_w, ada_b):
    depth = ada_w.shape[0]
    rows = cc.shape[0]
    col_tile = D_MODEL
    return pl.pallas_call(
        _mod_kernel,
        grid=(depth, (3 * D_MODEL) // col_tile),
        in_specs=[
            pl.BlockSpec((rows, D_MODEL), lambda l, n: (0, 0)),
            pl.BlockSpec((1, D_MODEL, col_tile), lambda l, n: (l, 0, n)),
            pl.BlockSpec((1, 1, col_tile), lambda l, n: (l, 0, n)),
        ],
        out_specs=pl.BlockSpec((1, rows, col_tile), lambda l, n: (l, 0, n)),
        out_shape=jax.ShapeDtypeStruct((depth, rows, 3 * D_MODEL), _F32),
        compiler_params=_params(2),
        name="modulation",
    )(cc, ada_w, ada_b.reshape(depth, 1, 3 * D_MODEL))


def _rms(x, w):
    ms = jnp.mean(x * x, axis=-1, keepdims=True)
    return x * lax.rsqrt(ms + RMS_EPS) * w


def _head_rms(x, w):
    lane = lax.broadcasted_iota(jnp.int32, x.shape, 1)
    y = x * x
    s = 1
    while s < HEAD_DIM:
        up = pltpu.roll(y, s, 1)
        dn = pltpu.roll(y, LANES - s, 1)
        y = y + jnp.where((lane & s) != 0, up, dn)
        s *= 2
    return x * lax.rsqrt(y * (1.0 / HEAD_DIM) + RMS_EPS) * w


def _rope(x, tab_ref, wide):
    if tab_ref is None:
        return x
    k, r = (0, HEAD_DIM // 4) if wide else (1, MLA_ROPE // 4)
    cos = tab_ref[:, (3 * k) * LANES:(3 * k + 1) * LANES]
    sinm = tab_ref[:, (3 * k + 1) * LANES:(3 * k + 2) * LANES]
    sinp = tab_ref[:, (3 * k + 2) * LANES:(3 * k + 3) * LANES]
    return x * cos + pltpu.roll(x, LANES - r, 1) * sinm + pltpu.roll(x, r, 1) * sinp


def _proj_kernel(*refs, has_rope, with_q, per_row_mod):
    refs = list(refs)
    x_ref, mod_ref, nw_ref = refs[:3]
    del refs[:3]
    tab_ref = refs.pop(0) if has_rope else None
    wkv_ref, wq_ref, wuq_ref, wukv_ref, qn_ref, kvn_ref, axq_ref, axk_ref = refs[:8]
    km_ref, vkv_ref = refs[8:10]

    nb, tile, _ = x_ref.shape

    def put(ref, cols, val):
        ref[:, :, cols] = val.reshape(nb, tile, val.shape[-1])

    mod = mod_ref[pl.program_id(1) if per_row_mod else 0]
    x = x_ref[...].reshape(nb * tile, D_MODEL)
    gain = nw_ref[...] * (1.0 + mod[:, D_MODEL:2 * D_MODEL])
    inv = lax.rsqrt(jnp.mean(x * x, axis=-1, keepdims=True) + RMS_EPS)
    h = (x * inv * gain + mod[:, :D_MODEL]).astype(_BF16)

    pkv = jnp.dot(h, wkv_ref[...], preferred_element_type=_F32)
    ckvn = _rms(pkv[:, :MLA_KV_RANK], kvn_ref[...]).astype(_BF16)
    kvm = jnp.dot(ckvn, wukv_ref[...], preferred_element_type=_F32)
    put(vkv_ref, slice(0, BRANCH_W), kvm[:, BRANCH_W:].astype(_BF16))
    o = MLA_KV_RANK
    kr2 = _rope(pkv[:, o:o + LANES], tab_ref, False).astype(_BF16)
    for p in range(N_PAIRS):
        kn = kvm[:, p * LANES:(p + 1) * LANES]
        lo, hi = p * MLA_QK_W, p * MLA_QK_W + LANES
        if has_rope:
            km_ref[0, lo:hi, :] = kn.T.astype(_BF16)
            km_ref[0, hi:hi + LANES, :] = kr2.T
        else:
            put(km_ref, slice(lo, hi), kn.astype(_BF16))
            put(km_ref, slice(hi, hi + LANES), kr2)
    o += LANES
    put(vkv_ref, slice(BRANCH_W, BRANCH_W + LANES), _rope(pkv[:, o:o + LANES], tab_ref, True).astype(_BF16))
    o += LANES
    put(vkv_ref, slice(BRANCH_W + LANES, BRANCH_W + 2 * LANES), pkv[:, o:o + LANES].astype(_BF16))
    o += LANES
    put(vkv_ref, slice(BRANCH_W + 2 * LANES, BRANCH_W + 3 * LANES),
        _rope(_head_rms(pkv[:, o:o + LANES], axk_ref[...]), tab_ref, True).astype(_BF16))
    o += LANES
    put(vkv_ref, slice(BRANCH_W + 3 * LANES, BRANCH_W + 4 * LANES), pkv[:, o:o + LANES].astype(_BF16))
    if not with_q:
        return

    q_ref, gz_ref = refs[10:]
    cq = jnp.dot(h, wq_ref[:, :MLA_Q_RANK], preferred_element_type=_F32)
    cqn = _rms(cq, qn_ref[...]).astype(_BF16)
    qm = jnp.dot(cqn, wuq_ref[...], preferred_element_type=_F32)
    for p in range(N_PAIRS):
        lo = p * MLA_QK_W
        put(q_ref, slice(lo, lo + LANES), (qm[:, lo:lo + LANES] * MLA_SCALE).astype(_BF16))
        put(q_ref, slice(lo + LANES, lo + MLA_QK_W),
            (_rope(qm[:, lo + LANES:lo + MLA_QK_W], tab_ref, False) * MLA_SCALE).astype(_BF16))
    c0 = MLA_Q_RANK
    sq = jnp.dot(h, wq_ref[:, c0:c0 + BRANCH_W], preferred_element_type=_F32)
    c0 += BRANCH_W
    aq = jnp.dot(h, wq_ref[:, c0:c0 + BRANCH_W], preferred_element_type=_F32)
    c0 += BRANCH_W
    for p in range(N_PAIRS):
        blk = slice(p * LANES, (p + 1) * LANES)
        qs0 = N_PAIRS * MLA_QK_W + p * LANES
        put(q_ref, slice(qs0, qs0 + LANES), (_rope(sq[:, blk], tab_ref, True) * GQA_SCALE).astype(_BF16))
        aqn = _head_rms(aq[:, blk], axq_ref[...])
        put(q_ref, slice(qs0 + BRANCH_W, qs0 + BRANCH_W + LANES), (_rope(aqn, tab_ref, True) * GQA_SCALE).astype(_BF16))
    for n in range(Z_W // BRANCH_W):
        z = jnp.dot(h, wq_ref[:, c0:c0 + BRANCH_W], preferred_element_type=_F32)
        put(gz_ref, slice(G_W + n * BRANCH_W, G_W + (n + 1) * BRANCH_W), z.astype(_BF16))
        c0 += BRANCH_W
    for n in range(G_W // BRANCH_W):
        g = jnp.dot(h, wq_ref[:, c0:c0 + BRANCH_W], preferred_element_type=_F32)
        put(gz_ref, slice(n * BRANCH_W, (n + 1) * BRANCH_W), g.astype(_BF16))
        c0 += BRANCH_W


def _project(x, mod, norm_w, tab, weights, *, tile, with_q, name, nb=1):
    B, N, D = x.shape
    assert B % nb == 0 and (nb == 1 or (mod.shape[0] == 1 and tab is None))
    widths = [N_PAIRS * MLA_QK_W, BRANCH_W + 4 * LANES]
    if with_q:
        widths += [N_PAIRS * MLA_QK_W + 2 * BRANCH_W, G_W + Z_W]
    in_specs = [pl.BlockSpec((nb, tile, D), lambda j, b: (b, j, 0)),
                _const_spec(mod.shape),
                _const_spec(norm_w.shape)]
    args = [x, mod, norm_w]
    if tab is not None:
        in_specs.append(pl.BlockSpec((tile, TAB_W), lambda j, b: (j, 0)))
        args.append(tab)
    in_specs += [_const_spec(w.shape) for w in weights]
    out_specs = [pl.BlockSpec((nb, tile, w), lambda j, b: (b, j, 0)) for w in widths]
    out_shape = [jax.ShapeDtypeStruct((B, N, w), _BF16) for w in widths]
    if tab is not None:
        out_specs[0] = pl.BlockSpec((1, widths[0], tile), lambda j, b: (b, 0, j))
        out_shape[0] = jax.ShapeDtypeStruct((B, widths[0], N), _BF16)
    return pl.pallas_call(
        functools.partial(_proj_kernel, has_rope=tab is not None, with_q=with_q, per_row_mod=mod.shape[0] == B),
        grid=(N // tile, B // nb),
        in_specs=in_specs,
        out_specs=out_specs,
        out_shape=out_shape,
        compiler_params=_params(2),
        name=name,
    )(*args, *weights)


def _scores(q, k):
    return lax.dot_general(q, k, (((1,), (1,)), ((), ())), preferred_element_type=_F32)


def _stack_heads(q, mla):
    lane = lax.broadcasted_iota(jnp.int32, q.shape, 1)
    is_a = lane < HEAD_DIM
    if mla:
        is_a = is_a | ((lane >= LANES) & (lane < LANES + MLA_ROPE))
    zero = jnp.zeros_like(q)
    return jnp.concatenate([jnp.where(is_a, q, zero), jnp.where(is_a, zero, q)], axis=0)


def _fill_v1(v1_scr, v_ref, bi, n_blocks):
    n = v_ref.shape[1]
    for j in range(n_blocks):
        v1_scr[j, :, :LANES] = v_ref[bi, :, j * LANES:(j + 1) * LANES]
        v1_scr[j, :, LANES:] = jnp.ones((n, LANES), _BF16)


def _online_softmax_pv(q2, key_tiles):
    m = acc = None
    for k, v1, _, k_is_transposed in key_tiles:
        s = jnp.dot(q2, k, preferred_element_type=_F32) if k_is_transposed else _scores(q2, k)
        m_new = jnp.max(s, axis=-1, keepdims=True)
        if m is not None:
            m_new = jnp.maximum(m, m_new)
        pv = jnp.dot(jnp.exp2(s - m_new).astype(_BF16), v1, preferred_element_type=_F32)
        acc = pv if acc is None else jnp.exp2(m - m_new) * acc + pv
        m = m_new
    return acc[:, :LANES] / acc[:, LANES:]


def _sink_softmax_pv(q2, key_tiles, sink):
    scores = []
    m = sink
    for k, _, bias, _ in key_tiles:
        s = _scores(q2, k)
        if bias is not None:
            s = s + bias
        scores.append(s)
        m = jnp.maximum(m, jnp.max(s, axis=-1, keepdims=True))
    acc = None
    for s, (_, v1, _, _) in zip(scores, key_tiles):
        pv = jnp.dot(jnp.exp2(s - m).astype(_BF16), v1, preferred_element_type=_F32)
        acc = pv if acc is None else acc + pv
    return acc[:, :LANES] / (acc[:, LANES:] + jnp.exp2(sink - m))


def _gate_store(o_ref, z_ref, bi, rows, p, r):
    n = r.shape[0] // 2
    lane = lax.broadcasted_iota(jnp.int32, (n, LANES), 1)
    o = jnp.where(lane < HEAD_DIM, r[:n], r[n:])
    blk = slice(p * LANES, (p + 1) * LANES)
    z = z_ref[bi, rows, blk].astype(_F32)
    o_ref[bi, rows, blk] = (o * (z / (1.0 + jnp.exp(-z)))).astype(_BF16)


def _tile_rows(tile):
    return pl.ds(pl.multiple_of(tile * Q_TILE, Q_TILE), Q_TILE)


def _attn_kernel(*refs, mla, window, latent_keys):
    refs = list(refs)
    sink_ref = refs.pop(0) if window else None
    q_ref, z_ref, kc_ref, vc_ref = refs[:4]
    k_ref, v_ref = refs[4:6] if latent_keys else (None, None)
    o_ref, vc1_scr = refs[-2 - latent_keys], refs[-1 - latent_keys]
    v1_scr = refs[-1] if latent_keys else None
    for bi in range(q_ref.shape[0]):
        _attn_row(bi, sink_ref, q_ref, z_ref, kc_ref, vc_ref, k_ref, v_ref, o_ref, vc1_scr, v1_scr,
                  mla=mla, window=window)


def _attn_row(bi, sink_ref, q_ref, z_ref, kc_ref, vc_ref, k_ref, v_ref, o_ref, vc1_scr, v1_scr, *, mla, window):
    latent_keys = k_ref is not None
    n_tiles = q_ref.shape[1] // Q_TILE
    wq = MLA_QK_W if mla else LANES
    n_blocks = N_PAIRS if mla else 1
    vc1_scr = vc1_scr.at[bi]
    _fill_v1(vc1_scr, vc_ref, bi, n_blocks)
    if latent_keys:
        _fill_v1(v1_scr, v_ref, bi, n_blocks)
        n_lat = v_ref.shape[1]

    def ctx_tile(p):
        kb = slice(p * wq, (p + 1) * wq) if mla else slice(None)
        return (kc_ref[bi, :, kb], vc1_scr[p if mla else 0], None, False)

    def latent_tile(p, start, bias):
        keys = pl.ds(start, KEY_TILE)
        if mla:
            return (k_ref[bi, p * wq:(p + 1) * wq, keys], v1_scr[p, keys, :], bias, True)
        return (k_ref[bi, keys, :], v1_scr[0, keys, :], bias, False)

    def pair_sink(p):
        row = lax.broadcasted_iota(jnp.int32, (2 * Q_TILE, 1), 0)
        return jnp.where(row < Q_TILE, sink_ref[p], sink_ref[N_PAIRS + p]) * LOG2E

    def step(t, carry):
        if window and latent_keys:
            q0 = t * Q_TILE
            k0 = jnp.clip(q0 - WINDOW, 0, n_lat - BAND)
            shape = (2 * Q_TILE, KEY_TILE)
            col = lax.broadcasted_iota(jnp.int32, shape, 1)
            row = lax.broadcasted_iota(jnp.int32, shape, 0) & (Q_TILE - 1)
            lat = [(pl.multiple_of(k0 + b0, LANES),
                    jnp.where(jnp.abs((col + (k0 + b0)) - (row + q0)) <= WINDOW, 0.0, NEG_INF))
                   for b0 in range(0, BAND, KEY_TILE)]
        elif latent_keys:
            lat = [(k0, None) for k0 in range(0, n_lat, KEY_TILE)]
        else:
            lat = []
        rows = _tile_rows(t)
        for p in range(N_PAIRS):
            tiles = [latent_tile(p, start, bias) for start, bias in lat] + [ctx_tile(p)]
            q2 = _stack_heads(q_ref[bi, rows, p * wq:(p + 1) * wq], mla)
            r = _sink_softmax_pv(q2, tiles, pair_sink(p)) if window else _online_softmax_pv(q2, tiles)
            _gate_store(o_ref, z_ref, bi, rows, p, r)
        return carry

    lax.fori_loop(0, n_tiles, step, 0, unroll=min(n_tiles, ATTN_UNROLL))


def _attention(q, z, kc, vc, k=None, v=None, *, mla=False, sink=None, name, nb=1):
    B, nq, _ = q[0].shape
    latent_keys = k is not None
    assert B % nb == 0 and (nb == 1 or not latent_keys)

    def view(operand):
        a, width, blk = operand
        if width is None:
            return pl.BlockSpec((nb,) + a.shape[1:], lambda b: (b, 0, 0))
        return pl.BlockSpec((nb, a.shape[1], width), lambda b: (b, 0, blk))

    operands = [q, z, kc, vc] + ([k, v] if latent_keys else [])
    args = [o[0] for o in operands]
    in_specs = [view(o) for o in operands]
    if sink is not None:
        args.insert(0, sink)
        in_specs.insert(0, pl.BlockSpec(memory_space=pltpu.SMEM))
    n_blocks = N_PAIRS if mla else 1
    scratch = [pltpu.VMEM((nb, n_blocks, CTX_LEN, 2 * LANES), _BF16)]
    if latent_keys:
        scratch.append(pltpu.VMEM((n_blocks, v[0].shape[1], 2 * LANES), _BF16))
    return pl.pallas_call(
        functools.partial(_attn_kernel, mla=mla, window=sink is not None, latent_keys=latent_keys),
        grid=(B // nb,),
        in_specs=in_specs,
        out_specs=pl.BlockSpec((nb, nq, BRANCH_W), lambda b: (b, 0, 0)),
        out_shape=jax.ShapeDtypeStruct((B, nq, BRANCH_W), _BF16),
        scratch_shapes=scratch,
        compiler_params=_params(1),
        name=name,
    )(*args)


def _merge_kernel(um_ref, us_ref, ua_ref, sg_ref, wom_ref, wos_ref, woa_ref, wout_ref,
                  x_ref, mod_ref, *rest, final_norm, per_row_mod):
    nb, tile, _ = x_ref.shape
    rows = nb * tile
    y = None
    for n, (u_ref, w_ref) in enumerate(((um_ref, wom_ref), (us_ref, wos_ref), (ua_ref, woa_ref))):
        yb = jnp.dot(u_ref[...].reshape(rows, BRANCH_W), w_ref[...], preferred_element_type=_F32)
        g = sg_ref[:, :, n * D_MODEL:(n + 1) * D_MODEL].reshape(rows, D_MODEL).astype(_F32)
        yb = yb / (1.0 + jnp.exp(-g))
        y = yb if y is None else y + yb
    out = jnp.dot(y.astype(_BF16), wout_ref[...], preferred_element_type=_F32)
    mod = mod_ref[pl.program_id(0) if per_row_mod else 0]
    xn = x_ref[...].reshape(rows, D_MODEL) + mod[:, 2 * D_MODEL:] * out
    if final_norm:
        fnw_ref, o_ref = rest
        o_ref[...] = _rms(xn, fnw_ref[...]).reshape(nb, tile, D_MODEL)
    else:
        (o_ref,) = rest
        o_ref[...] = xn.reshape(nb, tile, D_MODEL)


def _merge(us, gz, weights, x, mod, fnw=None, *, tile, name, nb=1):
    B, N, D = x.shape
    assert B % nb == 0 and (nb == 1 or mod.shape[0] == 1)
    tok = lambda w: pl.BlockSpec((nb, tile, w), lambda b, j: (b, j, 0))
    in_specs = ([tok(BRANCH_W)] * 3 + [tok(G_W)] + [_const_spec(w.shape) for w in weights]
                + [tok(D), _const_spec(mod.shape)])
    args = [*us, gz, *weights, x, mod]
    if fnw is not None:
        in_specs.append(_const_spec(fnw.shape))
        args.append(fnw)
    return pl.pallas_call(
        functools.partial(_merge_kernel, final_norm=fnw is not None, per_row_mod=mod.shape[0] == B),
        grid=(B // nb, N // tile),
        in_specs=in_specs,
        out_specs=tok(D),
        out_shape=jax.ShapeDtypeStruct((B, N, D), _F32),
        compiler_params=_params(2),
        name=name,
    )(*args)


def _axial_tables(d, n_lat):
    h = d // 2
    t = jnp.arange(n_lat, dtype=jnp.int32)
    pos_row = (t // GRID_W).astype(_F32)
    pos_col = (t % GRID_W).astype(_F32)
    freqs = ROPE_THETA ** (-jnp.arange(0, h, 2, dtype=_F32) / h)
    ang_r = pos_row[:, None] * freqs[None, :]
    ang_c = pos_col[:, None] * freqs[None, :]
    cr, sr, cc, sc = jnp.cos(ang_r), jnp.sin(ang_r), jnp.cos(ang_c), jnp.sin(ang_c)
    z = jnp.zeros_like(sr)
    cos = jnp.concatenate([cr, cr, cc, cc], axis=-1)
    sinm = jnp.concatenate([-sr, z, -sc, z], axis=-1)
    sinp = jnp.concatenate([z, sr, z, sc], axis=-1)
    return cos, sinm, sinp


def _rope_tables(n_lat):
    set64 = [jnp.tile(t, (1, 2)) for t in _axial_tables(HEAD_DIM, n_lat)]
    pad = LANES - 2 * MLA_ROPE
    set32 = [jnp.concatenate([t, t, jnp.full((n_lat, pad), f, _F32)], axis=1)
             for t, f in zip(_axial_tables(MLA_ROPE, n_lat), (1.0, 0.0, 0.0))]
    return jnp.concatenate(set64 + set32, axis=1)


def _permute_heads(w, axis):
    shape = w.shape
    w = w.reshape(shape[:axis] + (KV_HEADS, HEADS // KV_HEADS, HEAD_DIM) + shape[axis + 1:])
    return jnp.swapaxes(w, axis, axis + 1).reshape(shape)


def _layer_weights(w_in, w_uq, w_ukv, w_o_swa, w_o_ax):
    D = w_in.shape[0]
    offs = {}
    o = 0
    for name, w in (("ckv", MLA_KV_RANK), ("kr", MLA_ROPE), ("sk", LANES), ("sv", LANES),
                    ("ak", LANES), ("av", LANES), ("cq", MLA_Q_RANK), ("sq", BRANCH_W),
                    ("aq", BRANCH_W), ("zm", BRANCH_W), ("zs", BRANCH_W), ("za", BRANCH_W),
                    ("g", G_W)):
        offs[name] = w_in[:, o:o + w]
        o += w
    kr = offs["kr"]
    kr2 = jnp.concatenate([kr, kr, jnp.zeros((D, LANES - 2 * MLA_ROPE), w_in.dtype)], axis=1)
    wkv = jnp.concatenate([offs["ckv"], kr2, offs["sk"], offs["sv"], offs["ak"], offs["av"]], axis=1)
    wq = jnp.concatenate([offs["cq"], _permute_heads(offs["sq"], 1), _permute_heads(offs["aq"], 1),
                          offs["zm"], _permute_heads(offs["zs"], 1), _permute_heads(offs["za"], 1),
                          offs["g"]], axis=1)
    uq = w_uq.reshape(MLA_Q_RANK, N_PAIRS, 2, HEAD_DIM + MLA_ROPE)
    nope = uq[..., :HEAD_DIM].reshape(MLA_Q_RANK, N_PAIRS, 2 * HEAD_DIM)
    rope = uq[..., HEAD_DIM:].reshape(MLA_Q_RANK, N_PAIRS, 2 * MLA_ROPE)
    pad = jnp.zeros((MLA_Q_RANK, N_PAIRS, MLA_QK_W - 2 * HEAD_DIM - 2 * MLA_ROPE), w_uq.dtype)
    wuq = jnp.concatenate([nope, rope, pad], axis=-1).reshape(MLA_Q_RANK, N_PAIRS * MLA_QK_W)
    ukv = w_ukv.reshape(MLA_KV_RANK, HEADS, 2, HEAD_DIM)
    wukv = jnp.concatenate([ukv[:, :, 0].reshape(MLA_KV_RANK, BRANCH_W),
                            ukv[:, :, 1].reshape(MLA_KV_RANK, BRANCH_W)], axis=1)
    bf = lambda a: a.astype(_BF16)
    return (bf(wkv), bf(wq), bf(wuq), bf(wukv),
            bf(_permute_heads(w_o_swa, 0)), bf(_permute_heads(w_o_ax, 0)))


_GZ_Z_BLK = G_W // BRANCH_W
_VKV_KV_BLK = BRANCH_W // LANES


def _attend_all(sink, pq, pc, px=None, *, tag, nb=1):
    q, gz = pq[2], pq[3]
    z = lambda branch: (gz, BRANCH_W, _GZ_Z_BLK + branch)
    kv = lambda p, i: None if p is None else (p[1], LANES, _VKV_KV_BLK + i)
    mla_k = None if px is None else (px[0], None, 0)
    mla_v = None if px is None else (px[1], BRANCH_W, 0)
    return (_attention((q, N_PAIRS * MLA_QK_W, 0), z(0), (pc[0], None, 0), (pc[1], BRANCH_W, 0), mla_k, mla_v,
                       mla=True, name="attn_mla" + tag, nb=nb),
            _attention((q, BRANCH_W, 2), z(1), kv(pc, 0), kv(pc, 1), kv(px, 0), kv(px, 1),
                       sink=sink, name="attn_window" + tag, nb=nb),
            _attention((q, BRANCH_W, 3), z(2), kv(pc, 2), kv(pc, 3), kv(px, 2), kv(px, 3),
                       name="attn_axial" + tag, nb=nb))


def kernel(x, c, ctx, c_ctx, ada_w, ada_b, norm_w, w_in, mla_q_norm, mla_w_uq, mla_kv_norm, mla_w_ukv,
           swa_sink, ax_q_norm, ax_k_norm, w_o_mla, w_o_swa, w_o_ax, w_out, final_norm_w):
    B, T, D = x.shape
    depth = w_in.shape[0]
    merge_tile = min(MERGE_TILE, T)
    ctx_nb = CTX_ROWS_PER_STEP if B % CTX_ROWS_PER_STEP == 0 else 1
    ctx_proj_nb = CTX_PROJ_ROWS_PER_STEP if B % CTX_PROJ_ROWS_PER_STEP == 0 else 1
    assert (D, ctx.shape[1]) == (D_MODEL, CTX_LEN) and T % TOK_TILE == 0 and T % merge_tile == 0 and T >= BAND

    mod_rows = -(-(B + 1) // 8) * 8
    cc = jnp.concatenate([c, c_ctx[None, :], jnp.zeros((mod_rows - B - 1, D), c.dtype)], axis=0)
    mod = _modulation(cc, ada_w, ada_b)
    tab = _rope_tables(T)
    fnw = final_norm_w.reshape(1, D)
    two_heads = lambda w: jnp.tile(w, 2).reshape(1, LANES)

    for l in range(depth):
        update_ctx = l < depth - 1
        wkv, wq, wuq, wukv, wos, woa = _layer_weights(w_in[l], mla_w_uq[l], mla_w_ukv[l], w_o_swa[l], w_o_ax[l])
        proj_w = (wkv, wq, wuq, wukv, mla_q_norm[l].reshape(1, -1), mla_kv_norm[l].reshape(1, -1),
                  two_heads(ax_q_norm[l]), two_heads(ax_k_norm[l]))
        merge_w = (w_o_mla[l].astype(_BF16), wos, woa, w_out[l].astype(_BF16))
        nw = norm_w[l].reshape(1, D)
        modx = mod[l, :B].reshape(B, 1, 3 * D)
        modc = mod[l, B:B + 1].reshape(1, 1, 3 * D)

        pc = _project(ctx, modc, nw, None, proj_w, tile=CTX_LEN, with_q=update_ctx, name="project_ctx", nb=ctx_proj_nb)
        px = _project(x, modx, nw, tab, proj_w, tile=TOK_TILE, with_q=True, name="project")
        u3 = _attend_all(swa_sink[l], px, pc, px, tag="")
        if update_ctx:
            uc3 = _attend_all(swa_sink[l], pc, pc, tag="_ctx", nb=ctx_nb)
            ctx = _merge(uc3, pc[3], merge_w, ctx, modc, tile=CTX_LEN, name="merge_ctx", nb=ctx_nb)
            x = _merge(u3, px[3], merge_w, x, modx, tile=merge_tile, name="merge")
        else:
            x = _merge(u3, px[3], merge_w, x, modx, fnw, tile=merge_tile, name="merge_final")
    return x
```
